```python
import math
import jax, jax.numpy as jnp
from jax import lax
import numpy as np

D_MODEL = 2048
BATCH = 1
SEQ = 16384
DEPTH = 2

N_MEM = 256
RET_HEADS = 8
RET_DK = 128
RET_DV = 128
RET_CHUNK = 128
RET_W = RET_HEADS * RET_DV
DIFF_HEADS = 4
DIFF_DK = 128
DIFF_DV = 256
DIFF_QK_W = DIFF_HEADS * 2 * DIFF_DK
DIFF_W = DIFF_HEADS * DIFF_DV
Q_BLOCK = 128
MIX_WIDTH = RET_W + DIFF_W
IN_COLS = 4 * RET_W + 2 * DIFF_QK_W + DIFF_W
SPLIT_POINTS = (RET_W, 2 * RET_W, 3 * RET_W, 4 * RET_W,
                4 * RET_W + DIFF_QK_W, 4 * RET_W + 2 * DIFF_QK_W)
CONV_WIDTH = 31
XATTN_HEADS = 4
XATTN_DH = D_MODEL // XATTN_HEADS
D_FF = -(-8 * D_MODEL // (3 * 256)) * 256
N_EVEN = (DEPTH + 1) // 2
N_ODD = DEPTH // 2
DEEPNORM_ALPHA = (2.0 * DEPTH) ** 0.25
DEEPNORM_BETA = (8.0 * DEPTH) ** -0.25
LN_EPS = 1e-5

kernel_name = "hybrid_retention_diffattn_conformer_deepnorm"


def layer_norm(x, g, b):
    xf = x.astype(jnp.float32)
    mu = xf.mean(-1, keepdims=True)
    var = jnp.square(xf - mu).mean(-1, keepdims=True)
    return ((xf - mu) * lax.rsqrt(var + LN_EPS)).astype(x.dtype) * g + b


def head_layer_norm(x, g, b):
    B, S, H, d = x.shape
    xf = x.astype(jnp.float32)
    mu = xf.mean(-1, keepdims=True)
    var = jnp.square(xf - mu).mean(-1, keepdims=True)
    y = ((xf - mu) * lax.rsqrt(var + LN_EPS)).astype(x.dtype).reshape(B, S, H * d)
    return y * g + b


def head_rms_norm(x, g):
    B, S, H, d = x.shape
    xf = x.astype(jnp.float32)
    y = (xf * lax.rsqrt(jnp.mean(jnp.square(xf), -1, keepdims=True) + LN_EPS)).astype(x.dtype)
    return y.reshape(B, S, H * d) * g


def alibi_slopes(n_heads):
    return 2.0 ** (-8.0 * (jnp.arange(n_heads, dtype=jnp.float32) + 1.0) / n_heads)


def lambda_init_for(layer):
    return 0.8 - 0.6 * math.exp(-0.3 * layer)


def retention(q, k, v):
    B, S, H, dk = q.shape
    dv = v.shape[-1]
    C = RET_CHUNK
    nc = S // C
    dt = q.dtype
    log_gamma = jnp.log1p(-(2.0 ** (-5.0 - jnp.arange(H, dtype=jnp.float32))))
    k = k * (dk ** -0.5)
    q = q.reshape(B, nc, C, H, dk)
    k = k.reshape(B, nc, C, H, dk)
    v = v.reshape(B, nc, C, H, dv)
    pos = jnp.arange(C, dtype=jnp.float32)
    dist = pos[:, None] - pos[None, :]
    decay = jnp.where(dist >= 0,
                      jnp.exp(log_gamma[:, None, None] * jnp.maximum(dist, 0.0)),
                      0.0).astype(dt)
    scores = jnp.einsum('bnihd,bnjhd->bnhij', q, k) * decay
    y_intra = jnp.einsum('bnhij,bnjhe->bnihe', scores, v)
    zeta = jnp.exp(log_gamma[None, :] * (C - 1.0 - pos)[:, None]).astype(dt)
    kv = jnp.einsum('bnjhd,bnjhe->bnhde', k * zeta[:, :, None], v)
    chunk_decay = jnp.exp(log_gamma * C).astype(dt)[:, None, None]

    def step(state, kv_n):
        return state * chunk_decay + kv_n, state

    _, prev = lax.scan(step, jnp.zeros_like(kv[:, 0]), jnp.moveaxis(kv, 1, 0))
    prev = jnp.moveaxis(prev, 0, 1)
    xi = jnp.exp(log_gamma[None, :] * (pos + 1.0)[:, None]).astype(dt)
    y_cross = jnp.einsum('bnihd,bnhde->bnihe', q * xi[:, :, None], prev)
    return (y_intra + y_cross).reshape(B, S, H, dv)


def diff_attention(q, k, v, lam, slopes):
    B, S, H, _, dk = q.shape
    dv = v.shape[-1]
    nb = S // Q_BLOCK
    scale = dk ** -0.5
    qb = jnp.moveaxis(q.reshape(B, nb, Q_BLOCK, H, 2, dk), 1, 0)
    kpos = jnp.arange(S)

    def one_block(args):
        q_blk, blk = args
        qpos = blk * Q_BLOCK + jnp.arange(Q_BLOCK)
        s = jnp.einsum('bqhmd,bkhmd->bhmqk', q_blk, k).astype(jnp.float32) * scale
        dist = (qpos[:, None] - kpos[None, :]).astype(jnp.float32)
        s = s - slopes[None, :, None, None, None] * dist
        s = jnp.where(dist >= 0, s, -jnp.inf)
        p = jax.nn.softmax(s, axis=-1)
        a = p[:, :, 0] - lam * p[:, :, 1]
        return jnp.einsum('bhqk,bkhe->bqhe', a.astype(v.dtype), v)

    out = lax.map(one_block, (qb, jnp.arange(nb)))
    return jnp.moveaxis(out, 0, 1).reshape(B, S, H, dv)


def retention_diffattn_mixer(x, w_in, ret_gn_g, ret_gn_b, diff_lambda, diff_subln_g,
                             w_out, lambda_init):
    B, S, _ = x.shape
    proj = x @ w_in
    rq, rk, rv, rg, dq, dk_, dv_ = jnp.split(proj, SPLIT_POINTS, axis=-1)
    ret = retention(rq.reshape(B, S, RET_HEADS, RET_DK),
                    rk.reshape(B, S, RET_HEADS, RET_DK),
                    rv.reshape(B, S, RET_HEADS, RET_DV))
    ret = jax.nn.silu(rg) * head_layer_norm(ret, ret_gn_g, ret_gn_b)
    lf = diff_lambda.astype(jnp.float32)
    lam = jnp.exp(jnp.sum(lf[0] * lf[1])) - jnp.exp(jnp.sum(lf[2] * lf[3])) + lambda_init
    d = diff_attention(dq.reshape(B, S, DIFF_HEADS, 2, DIFF_DK),
                       dk_.reshape(B, S, DIFF_HEADS, 2, DIFF_DK),
                       dv_.reshape(B, S, DIFF_HEADS, DIFF_DV),
                       lam, alibi_slopes(DIFF_HEADS))
    d = head_rms_norm(d, diff_subln_g) * (1.0 - lambda_init)
    return jnp.concatenate([ret, d], axis=-1) @ w_out


def conformer_conv(x, w_pw1, b_pw1, w_dw, b_dw, ln_g, ln_b, w_pw2, b_pw2):
    D = x.shape[-1]
    h = x @ w_pw1 + b_pw1
    h = h[..., :D] * jax.nn.sigmoid(h[..., D:])
    h = lax.conv_general_dilated(
        h, w_dw[:, None, :].astype(h.dtype), window_strides=(1,),
        padding=[(CONV_WIDTH - 1, 0)],
        dimension_numbers=('NWC', 'WIO', 'NWC'),
        feature_group_count=D) + b_dw
    h = jax.nn.silu(layer_norm(h, ln_g, ln_b))
    return h @ w_pw2 + b_pw2


def memory_cross_attention(x, mem, wq, wk, wv, wo):
    B, S, D = x.shape
    M = mem.shape[1]
    q = (x @ wq).reshape(B, S, XATTN_HEADS, XATTN_DH)
    k = (mem @ wk).reshape(B, M, XATTN_HEADS, XATTN_DH)
    v = (mem @ wv).reshape(B, M, XATTN_HEADS, XATTN_DH)
    s = jnp.einsum('bshd,bmhd->bhsm', q, k).astype(jnp.float32) * (XATTN_DH ** -0.5)
    p = jax.nn.softmax(s, axis=-1).astype(v.dtype)
    o = jnp.einsum('bhsm,bmhd->bshd', p, v).reshape(B, S, D)
    return o @ wo


def swiglu_ffn(x, w_in, w_out):
    h = x @ w_in
    return (jax.nn.silu(h[..., :D_FF]) * h[..., D_FF:]) @ w_out


def setup_inputs(seed: int = 0) -> dict:
    key = jax.random.key(seed)
    ks = iter(jax.random.split(key, 32))

    def nrm(shape, scale):
        return jax.random.normal(next(ks), shape, jnp.float32) * scale

    def gain(shape):
        return 1.0 + nrm(shape, 0.02)

    D = D_MODEL
    return {
        "x": nrm((BATCH, SEQ, D), 1.0),
        "mem": nrm((BATCH, N_MEM, D), 1.0),
        "w_in": nrm((N_EVEN, D, IN_COLS), D ** -0.5),
        "ret_gn_g": gain((N_EVEN, RET_W)),
        "ret_gn_b": nrm((N_EVEN, RET_W), 0.02),
        "diff_lambda": nrm((N_EVEN, 4, DIFF_DK), 0.1),
        "diff_subln_g": gain((N_EVEN, DIFF_W)),
        "w_mix_out": nrm((N_EVEN, MIX_WIDTH, D), MIX_WIDTH ** -0.5 * DEEPNORM_BETA),
        "conv_w_pw1": nrm((N_ODD, D, 2 * D), D ** -0.5),
        "conv_b_pw1": nrm((N_ODD, 2 * D), 0.02),
        "conv_w_dw": nrm((N_ODD, CONV_WIDTH, D), CONV_WIDTH ** -0.5),
        "conv_b_dw": nrm((N_ODD, D), 0.02),
        "conv_ln_g": gain((N_ODD, D)),
        "conv_ln_b": nrm((N_ODD, D), 0.02),
        "conv_w_pw2": nrm((N_ODD, D, D), D ** -0.5 * DEEPNORM_BETA),
        "conv_b_pw2": nrm((N_ODD, D), 0.02),
        "xattn_wq": nrm((DEPTH, D, D), D ** -0.5),
        "xattn_wk": nrm((DEPTH, D, D), D ** -0.5),
        "xattn_wv": nrm((DEPTH, D, D), D ** -0.5),
        "xattn_wo": nrm((DEPTH, D, D), D ** -0.5 * DEEPNORM_BETA),
        "ffn_w_in": nrm((DEPTH, D, 2 * D_FF), D ** -0.5),
        "ffn_w_out": nrm((DEPTH, D_FF, D), D_FF ** -0.5 * DEEPNORM_BETA),
        "ln_g": gain((DEPTH, 3, D)),
        "ln_b": nrm((DEPTH, 3, D), 0.02),
    }


def reference(x, mem, w_in, ret_gn_g, ret_gn_b, diff_lambda, diff_subln_g, w_mix_out,
              conv_w_pw1, conv_b_pw1, conv_w_dw, conv_b_dw, conv_ln_g, conv_ln_b,
              conv_w_pw2, conv_b_pw2, xattn_wq, xattn_wk, xattn_wv, xattn_wo,
              ffn_w_in, ffn_w_out, ln_g, ln_b):
    h = x
    for layer in range(DEPTH):
        if layer % 2 == 0:
            e = layer // 2
            mix = retention_diffattn_mixer(h, w_in[e], ret_gn_g[e], ret_gn_b[e],
                                           diff_lambda[e], diff_subln_g[e], w_mix_out[e],
                                           lambda_init_for(layer + 1))
        else:
            o = layer // 2
            mix = conformer_conv(h, conv_w_pw1[o], conv_b_pw1[o], conv_w_dw[o], conv_b_dw[o],
                                 conv_ln_g[o], conv_ln_b[o], conv_w_pw2[o], conv_b_pw2[o])
        h = layer_norm(DEEPNORM_ALPHA * h + mix, ln_g[layer, 0], ln_b[layer, 0])
        xa = memory_cross_attention(h, mem, xattn_wq[layer], xattn_wk[layer],
                                    xattn_wv[layer], xattn_wo[layer])
        h = layer_norm(DEEPNORM_ALPHA * h + xa, ln_g[layer, 1], ln_b[layer, 1])
        ff = swiglu_ffn(h, ffn_w_in[layer], ffn_w_out[layer])
        h = layer_norm(DEEPNORM_ALPHA * h + ff, ln_g[layer, 2], ln_b[layer, 2])
    return h
```

```python
import functools
import math

import jax
import jax.numpy as jnp
from jax import lax
from jax.experimental import pallas as pl
from jax.experimental.pallas import tpu as pltpu

F32 = jnp.float32
BF16 = jnp.bfloat16

D_MODEL = 2048
DEPTH = 2
RET_HEADS = 8
RET_DK = 128
RET_CHUNK = 128
RET_W = 1024
DIFF_HEADS = 4
DIFF_DK = 128
DIFF_DV = 256
DIFF_W = 1024
CONV_WIDTH = 31
CONV_HALO = 32
XATTN_HEADS = 4
XATTN_DH = D_MODEL // XATTN_HEADS
D_FF = 5632
ALPHA = (2.0 * DEPTH) ** 0.25
LN_EPS = 1e-5
NEG_BIG = -1e30

VMEM_LIMIT_BYTES = 56 * 1024 * 1024

_NT = (((1,), (1,)), ((), ()))
_TN = (((0,), (0,)), ((), ()))


def _params(*sem):
    return pltpu.CompilerParams(dimension_semantics=sem, vmem_limit_bytes=VMEM_LIMIT_BYTES)


def _dot(a, b):
    return jnp.dot(a, b, preferred_element_type=F32)


def _layer_norm(z, g, b):
    mu = jnp.mean(z, axis=-1, keepdims=True)
    d = z - mu
    var = jnp.mean(d * d, axis=-1, keepdims=True)
    return d * lax.rsqrt(var + LN_EPS) * g + b


def _silu(x):
    return x * jax.nn.sigmoid(x)


def _mm_kernel(x_ref, w_ref, o_ref):
    o_ref[...] = _dot(x_ref[...], w_ref[...]).astype(o_ref.dtype)


def _matmul(x, w, out_dtype, tm=1024, tn=1024):
    M, K = x.shape
    N = w.shape[1]
    tm, tn = min(tm, M), min(tn, N)
    return pl.pallas_call(
        _mm_kernel,
        grid=(M // tm, N // tn),
        in_specs=[pl.BlockSpec((tm, K), lambda i, j: (i, 0)),
                  pl.BlockSpec((K, tn), lambda i, j: (0, j))],
        out_specs=pl.BlockSpec((tm, tn), lambda i, j: (i, j)),
        out_shape=jax.ShapeDtypeStruct((M, N), out_dtype),
        compiler_params=_params("parallel", "arbitrary"),
    )(x, w)


def _mm_glu_kernel(x_ref, wa_ref, wb_ref, ba_ref, bb_ref, o_ref):
    x = x_ref[...]
    a = _dot(x, wa_ref[...]) + ba_ref[...]
    b = _dot(x, wb_ref[...]) + bb_ref[...]
    o_ref[...] = a * jax.nn.sigmoid(b)


def _matmul_glu(x, w, bias, tm=1024, tn=512):
    M, K = x.shape
    N = w.shape[1] // 2
    nb = N // tn
    bias2 = bias.reshape(1, 2 * N)
    return pl.pallas_call(
        _mm_glu_kernel,
        grid=(M // tm, nb),
        in_specs=[pl.BlockSpec((tm, K), lambda i, j: (i, 0)),
                  pl.BlockSpec((K, tn), lambda i, j: (0, j)),
                  pl.BlockSpec((K, tn), lambda i, j: (0, j + nb)),
                  pl.BlockSpec((1, tn), lambda i, j: (0, j)),
                  pl.BlockSpec((1, tn), lambda i, j: (0, j + nb))],
        out_specs=pl.BlockSpec((tm, tn), lambda i, j: (i, j)),
        out_shape=jax.ShapeDtypeStruct((M, N), F32),
        compiler_params=_params("parallel", "arbitrary"),
    )(x, w, w, bias2, bias2)


def _mm_ln_kernel(*refs, n_pairs, emit_bf16):
    xs = refs[:n_pairs]
    ws = refs[n_pairs:2 * n_pairs]
    bias_ref, r_ref, g_ref, b_ref = refs[2 * n_pairs:2 * n_pairs + 4]
    outs = refs[2 * n_pairs + 4:]
    acc = _dot(xs[0][...], ws[0][...])
    for x_ref, w_ref in zip(xs[1:], ws[1:]):
        acc = acc + _dot(x_ref[...], w_ref[...])
    z = ALPHA * r_ref[...] + acc + bias_ref[...]
    y = _layer_norm(z, g_ref[...], b_ref[...])
    outs[0][...] = y
    if emit_bf16:
        outs[1][...] = y.astype(BF16)


def _matmul_ln(xs, ws, bias, resid, g, b, emit_bf16=True, tm=512):
    M, N = resid.shape
    n = len(xs)
    row = lambda i: (i, 0)
    fixed = lambda i: (0, 0)
    in_specs = ([pl.BlockSpec((tm, x.shape[1]), row) for x in xs]
                + [pl.BlockSpec(w.shape, fixed) for w in ws]
                + [pl.BlockSpec((1, N), fixed), pl.BlockSpec((tm, N), row),
                   pl.BlockSpec((1, N), fixed), pl.BlockSpec((1, N), fixed)])
    out_specs = [pl.BlockSpec((tm, N), row)]
    out_shape = [jax.ShapeDtypeStruct((M, N), F32)]
    if emit_bf16:
        out_specs.append(pl.BlockSpec((tm, N), row))
        out_shape.append(jax.ShapeDtypeStruct((M, N), BF16))
    outs = pl.pallas_call(
        functools.partial(_mm_ln_kernel, n_pairs=n, emit_bf16=emit_bf16),
        grid=(M // tm,),
        in_specs=in_specs, out_specs=out_specs, out_shape=out_shape,
        compiler_params=_params("parallel"),
    )(*xs, *ws, bias.reshape(1, N), resid, g.reshape(1, N), b.reshape(1, N))
    return outs if emit_bf16 else (outs[0], None)


def _ffn_kernel(x_ref, wg_ref, wu_ref, wo_ref, r_ref, g_ref, b_ref, *rest, nf, emit_bf16):
    outs, acc_ref = rest[:-1], rest[-1]
    f = pl.program_id(1)
    x = x_ref[...]
    hg = _dot(x, wg_ref[...])
    hu = _dot(x, wu_ref[...])
    a = (_silu(hg) * hu).astype(BF16)
    contrib = _dot(a, wo_ref[...])

    @pl.when(f == 0)
    def _():
        acc_ref[...] = contrib

    @pl.when(f > 0)
    def _():
        acc_ref[...] += contrib

    @pl.when(f == nf - 1)
    def _():
        z = ALPHA * r_ref[...] + acc_ref[...]
        y = _layer_norm(z, g_ref[...], b_ref[...])
        outs[0][...] = y
        if emit_bf16:
            outs[1][...] = y.astype(BF16)


def _ffn(xb, w_in, w_out, resid, g, b, emit_bf16=True, tm=512, tf=512):
    M, D = resid.shape
    nf = D_FF // tf
    row = lambda i, f: (i, 0)
    fixed = lambda i, f: (0, 0)
    out_specs = [pl.BlockSpec((tm, D), row)]
    out_shape = [jax.ShapeDtypeStruct((M, D), F32)]
    if emit_bf16:
        out_specs.append(pl.BlockSpec((tm, D), row))
        out_shape.append(jax.ShapeDtypeStruct((M, D), BF16))
    outs = pl.pallas_call(
        functools.partial(_ffn_kernel, nf=nf, emit_bf16=emit_bf16),
        grid=(M // tm, nf),
        in_specs=[pl.BlockSpec((tm, D), row),
                  pl.BlockSpec((D, tf), lambda i, f: (0, f)),
                  pl.BlockSpec((D, tf), lambda i, f: (0, f + nf)),
                  pl.BlockSpec((tf, D), lambda i, f: (f, 0)),
                  pl.BlockSpec((tm, D), row),
                  pl.BlockSpec((1, D), fixed), pl.BlockSpec((1, D), fixed)],
        out_specs=out_specs, out_shape=out_shape,
        scratch_shapes=[pltpu.VMEM((tm, D), F32)],
        compiler_params=_params("parallel", "arbitrary"),
    )(xb, w_in, w_in, w_out, resid, g.reshape(1, D), b.reshape(1, D))
    return outs if emit_bf16 else (outs[0], None)


def _ret_kernel(q_ref, k_ref, v_ref, gate_ref, gg_ref, gb_ref, o_ref, state_ref, *, nch):
    C = RET_CHUNK

    @pl.when(pl.program_id(0) == 0)
    def _():
        state_ref[...] = jnp.zeros_like(state_ref)

    scale = RET_DK ** -0.5
    row = lax.broadcasted_iota(jnp.int32, (C, C), 0).astype(F32)
    col = lax.broadcasted_iota(jnp.int32, (C, C), 1).astype(F32)
    dist = row - col
    for h in range(RET_HEADS):
        log_gamma = math.log1p(-(2.0 ** (-5.0 - h)))
        decay = jnp.where(dist >= 0, jnp.exp(log_gamma * jnp.maximum(dist, 0.0)), 0.0) * scale
        xi = jnp.exp(log_gamma * (row + 1.0))
        zeta = jnp.exp(log_gamma * (C - 1.0 - row)) * scale
        chunk_decay = math.exp(log_gamma * C)
        cs = slice(h * RET_DK, (h + 1) * RET_DK)
        gg = gg_ref[:, cs]
        gb = gb_ref[:, cs]
        for c in range(nch):
            rs = slice(c * C, (c + 1) * C)
            q = q_ref[rs, cs]
            k = k_ref[rs, cs]
            v = v_ref[rs, cs]
            state = state_ref[h]
            scores = lax.dot_general(q, k, _NT, preferred_element_type=F32) * decay
            y = _dot(scores.astype(BF16), v) + xi * _dot(q, state.astype(BF16))
            kz = (k.astype(F32) * zeta).astype(BF16)
            kv = lax.dot_general(kz, v, _TN, preferred_element_type=F32)
            state_ref[h] = state * chunk_decay + kv
            mu = jnp.mean(y, axis=-1, keepdims=True)
            d = y - mu
            var = jnp.mean(d * d, axis=-1, keepdims=True)
            yn = d * lax.rsqrt(var + LN_EPS) * gg + gb
            o_ref[rs, cs] = (_silu(gate_ref[rs, cs]) * yn).astype(BF16)


def _retention(qkv, gate, gn_g, gn_b, tr=512):
    S = qkv.shape[0]
    tr = min(tr, S)
    fixed = lambda i: (0, 0)
    return pl.pallas_call(
        functools.partial(_ret_kernel, nch=tr // RET_CHUNK),
        grid=(S // tr,),
        in_specs=[pl.BlockSpec((tr, RET_W), lambda i: (i, 0)),
                  pl.BlockSpec((tr, RET_W), lambda i: (i, 1)),
                  pl.BlockSpec((tr, RET_W), lambda i: (i, 2)),
                  pl.BlockSpec((tr, RET_W), lambda i: (i, 0)),
                  pl.BlockSpec((1, RET_W), fixed), pl.BlockSpec((1, RET_W), fixed)],
        out_specs=pl.BlockSpec((tr, RET_W), lambda i: (i, 0)),
        out_shape=jax.ShapeDtypeStruct((S, RET_W), BF16),
        scratch_shapes=[pltpu.VMEM((RET_HEADS, RET_DK, RET_DK), F32)],
        compiler_params=_params("arbitrary"),
    )(qkv, qkv, qkv, gate, gn_g.reshape(1, RET_W), gn_b.reshape(1, RET_W))


def _diff_kernel(lam_ref, q_ref, k_ref, v_ref, g_ref, o_ref, m_ref, l_ref, acc_ref, *, tq, lambda_init):
    h = pl.program_id(0)
    i = pl.program_id(1)
    tk = tq
    scale = DIFF_DK ** -0.5
    m_ref[...] = jnp.full(m_ref.shape, NEG_BIG, F32)
    l_ref[...] = jnp.zeros_like(l_ref)
    acc_ref[...] = jnp.zeros_like(acc_ref)

    head = (h + 1).astype(F32)
    slope = jnp.exp2(jnp.full((1, tk), -8.0 / DIFF_HEADS, F32) * head)
    kcol = lax.broadcasted_iota(jnp.int32, (1, tk), 1).astype(F32)

    def tile(j, masked):
        start = pl.multiple_of(j * tk, tk)
        ks = k_ref[pl.ds(start, tk), :]
        vs = v_ref[pl.ds(start, tk), :]
        bias = slope * (kcol + ((j - i) * tk).astype(F32))
        for mi in range(2):
            ms = slice(mi * DIFF_DK, (mi + 1) * DIFF_DK)
            s = lax.dot_general(q_ref[:, ms], ks[:, ms], _NT, preferred_element_type=F32) * scale + bias
            if masked:
                r = lax.broadcasted_iota(jnp.int32, (tq, tk), 0)
                c = lax.broadcasted_iota(jnp.int32, (tq, tk), 1)
                s = jnp.where(r >= c, s, NEG_BIG)
            m_old = m_ref[mi]
            m_new = jnp.maximum(m_old, jnp.max(s, axis=-1, keepdims=True))
            p = jnp.exp(s - m_new)
            alpha = jnp.exp(m_old - m_new)
            l_ref[mi] = alpha * l_ref[mi] + jnp.sum(p, axis=-1, keepdims=True)
            acc_ref[mi] = alpha * acc_ref[mi] + _dot(p.astype(BF16), vs)
            m_ref[mi] = m_new

    def body(j, carry):
        tile(j, masked=False)
        return carry

    lax.fori_loop(0, i, body, 0)
    tile(i, masked=True)

    lf = lam_ref[...]
    lam = (jnp.exp(jnp.sum(lf[0:1] * lf[1:2], axis=-1, keepdims=True))
           - jnp.exp(jnp.sum(lf[2:3] * lf[3:4], axis=-1, keepdims=True)) + lambda_init)
    o = acc_ref[0] * (1.0 / l_ref[0]) - lam * (acc_ref[1] * (1.0 / l_ref[1]))
    y = o * lax.rsqrt(jnp.mean(o * o, axis=-1, keepdims=True) + LN_EPS)
    o_ref[...] = (y * g_ref[...] * (1.0 - lambda_init)).astype(BF16)


def _diff_attention(qkv, diff_lambda, subln_g, lambda_init, tq=512):
    S = qkv.shape[0]
    tq = min(tq, S)
    qoff = 3 * RET_W // DIFF_DV
    koff = qoff + DIFF_W // DIFF_DV
    voff = koff + DIFF_W // DIFF_DV
    return pl.pallas_call(
        functools.partial(_diff_kernel, tq=tq, lambda_init=lambda_init),
        grid=(DIFF_HEADS, S // tq),
        in_specs=[pl.BlockSpec((4, DIFF_DK), lambda h, i: (0, 0)),
                  pl.BlockSpec((tq, DIFF_DV), lambda h, i: (i, qoff + h)),
                  pl.BlockSpec((S, DIFF_DV), lambda h, i: (0, koff + h)),
                  pl.BlockSpec((S, DIFF_DV), lambda h, i: (0, voff + h)),
                  pl.BlockSpec((1, DIFF_DV), lambda h, i: (0, h))],
        out_specs=pl.BlockSpec((tq, DIFF_DV), lambda h, i: (i, h)),
        out_shape=jax.ShapeDtypeStruct((S, DIFF_W), BF16),
        scratch_shapes=[pltpu.VMEM((2, tq, 1), F32), pltpu.VMEM((2, tq, 1), F32),
                        pltpu.VMEM((2, tq, DIFF_DV), F32)],
        compiler_params=_params("parallel", "arbitrary"),
    )(diff_lambda, qkv, qkv, qkv, subln_g.reshape(1, DIFF_W))


def _conv_kernel(cur_ref, halo_ref, w_ref, bdw_ref, g_ref, b_ref, o_ref, ext_ref, y_ref, *, tm):
    i = pl.program_id(0)
    ext_ref[0:CONV_HALO, :] = jnp.where(i > 0, halo_ref[...], 0.0)
    ext_ref[CONV_HALO:, :] = cur_ref[...]
    first = CONV_HALO - (CONV_WIDTH - 1)
    D = cur_ref.shape[1]
    for c in range(D // 128):
        cs = slice(c * 128, (c + 1) * 128)
        acc = jnp.broadcast_to(bdw_ref[:, cs], (tm, 128))
        for j in range(CONV_WIDTH):
            acc = acc + w_ref[j:j + 1, cs] * ext_ref[first + j:first + j + tm, cs]
        y_ref[:, cs] = acc
    y = _layer_norm(y_ref[...], g_ref[...], b_ref[...])
    o_ref[...] = _silu(y).astype(BF16)


def _conv_ln_silu(u, w_dw, b_dw, g, b, tm=128):
    S, D = u.shape
    per = tm // CONV_HALO
    fixed = lambda i: (0, 0)
    return pl.pallas_call(
        functools.partial(_conv_kernel, tm=tm),
        grid=(S // tm,),
        in_specs=[pl.BlockSpec((tm, D), lambda i: (i, 0)),
                  pl.BlockSpec((CONV_HALO, D), lambda i: (jnp.maximum(i * per - 1, 0), 0)),
                  pl.BlockSpec((CONV_WIDTH, D), fixed),
                  pl.BlockSpec((1, D), fixed), pl.BlockSpec((1, D), fixed), pl.BlockSpec((1, D), fixed)],
        out_specs=pl.BlockSpec((tm, D), lambda i: (i, 0)),
        out_shape=jax.ShapeDtypeStruct((S, D), BF16),
        scratch_shapes=[pltpu.VMEM((tm + CONV_HALO, D), F32), pltpu.VMEM((tm, D), F32)],
        compiler_params=_params("parallel"),
    )(u, u, w_dw, b_dw.reshape(1, D), g.reshape(1, D), b.reshape(1, D))


def _xattn_kernel(q_ref, k_ref, v_ref, wo_ref, r_ref, g_ref, b_ref, of_ref, ob_ref):
    scale = XATTN_DH ** -0.5
    heads = []
    for h in range(XATTN_HEADS):
        cs = slice(h * XATTN_DH, (h + 1) * XATTN_DH)
        s = lax.dot_general(q_ref[:, cs], k_ref[:, cs], _NT, preferred_element_type=F32) * scale
        m = jnp.max(s, axis=-1, keepdims=True)
        p = jnp.exp(s - m)
        p = p * (1.0 / jnp.sum(p, axis=-1, keepdims=True))
        heads.append(_dot(p.astype(BF16), v_ref[:, cs]).astype(BF16))
    o = jnp.concatenate(heads, axis=-1)
    z = ALPHA * r_ref[...] + _dot(o, wo_ref[...])
    y = _layer_norm(z, g_ref[...], b_ref[...])
    of_ref[...] = y
    ob_ref[...] = y.astype(BF16)


def _xattn(q, k, v, wo, resid, g, b, tm=512):
    M, D = resid.shape
    row = lambda i: (i, 0)
    fixed = lambda i: (0, 0)
    return pl.pallas_call(
        _xattn_kernel,
        grid=(M // tm,),
        in_specs=[pl.BlockSpec((tm, D), row),
                  pl.BlockSpec(k.shape, fixed), pl.BlockSpec(v.shape, fixed),
                  pl.BlockSpec((D, D), fixed),
                  pl.BlockSpec((tm, D), row),
                  pl.BlockSpec((1, D), fixed), pl.BlockSpec((1, D), fixed)],
        out_specs=[pl.BlockSpec((tm, D), row), pl.BlockSpec((tm, D), row)],
        out_shape=[jax.ShapeDtypeStruct((M, D), F32), jax.ShapeDtypeStruct((M, D), BF16)],
        compiler_params=_params("parallel"),
    )(q, k, v, wo, resid, g.reshape(1, D), b.reshape(1, D))


def _lambda_init_for(layer):
    return 0.8 - 0.6 * math.exp(-0.3 * layer)


def kernel(x, mem, w_in, ret_gn_g, ret_gn_b, diff_lambda, diff_subln_g, w_mix_out, conv_w_pw1, conv_b_pw1, conv_w_dw, conv_b_dw, conv_ln_g, conv_ln_b, conv_w_pw2, conv_b_pw2, xattn_wq, xattn_wk, xattn_wv, xattn_wo, ffn_w_in, ffn_w_out, ln_g, ln_b):
    B, S, D = x.shape
    assert B == 1 and D == D_MODEL
    h = x[0]
    hb = h.astype(BF16)
    memb = mem[0].astype(BF16)
    zero_bias = jnp.zeros((D,), F32)
    for layer in range(DEPTH):
        last = layer == DEPTH - 1
        if layer % 2 == 0:
            e = layer // 2
            w = w_in[e]
            gate_lo, gate_hi = 3 * RET_W, 4 * RET_W
            w_qkv = jnp.concatenate([w[:, :gate_lo], w[:, gate_hi:]], axis=1).astype(BF16)
            w_gate = w[:, gate_lo:gate_hi].astype(BF16)
            qkv = _matmul(hb, w_qkv, BF16)
            gate = _matmul(hb, w_gate, F32)
            ret = _retention(qkv, gate, ret_gn_g[e], ret_gn_b[e])
            dif = _diff_attention(qkv, diff_lambda[e], diff_subln_g[e], _lambda_init_for(layer + 1))
            wo = w_mix_out[e].astype(BF16)
            h, hb = _matmul_ln([ret, dif], [wo[:RET_W], wo[RET_W:]], zero_bias, h,
                               ln_g[layer, 0], ln_b[layer, 0])
        else:
            o = layer // 2
            u = _matmul_glu(hb, conv_w_pw1[o].astype(BF16), conv_b_pw1[o])
            c = _conv_ln_silu(u, conv_w_dw[o], conv_b_dw[o], conv_ln_g[o], conv_ln_b[o])
            h, hb = _matmul_ln([c], [conv_w_pw2[o].astype(BF16)], conv_b_pw2[o], h,
                               ln_g[layer, 0], ln_b[layer, 0])
        q = _matmul(hb, xattn_wq[layer].astype(BF16), BF16)
        k = _matmul(memb, xattn_wk[layer].astype(BF16), BF16)
        v = _matmul(memb, xattn_wv[layer].astype(BF16), BF16)
        h, hb = _xattn(q, k, v, xattn_wo[layer].astype(BF16), h, ln_g[layer, 1], ln_b[layer, 1])
        h, hb = _ffn(hb, ffn_w_in[layer].astype(BF16), ffn_w_out[layer].astype(BF16), h,
                     ln_g[layer, 2], ln_b[layer, 2], emit_bf16=not last)
    return h[None]
```

```python
import functools
import math

import jax
import jax.numpy as jnp
from jax import lax
from jax.experimental import pallas as pl
from jax.experimental.pallas import tpu as pltpu

F32 = jnp.float32
BF16 = jnp.bfloat16

D_MODEL = 2048
DEPTH = 2
RET_HEADS = 8
RET_DK = 128
RET_CHUNK = 128
RET_W = 1024
DIFF_HEADS = 4
DIFF_DK = 128
DIFF_DV = 256
DIFF_W = 1024
DIFF_ROW_CHUNK = 128
CONV_WIDTH = 31
CONV_HALO = 32
XATTN_HEADS = 4
XATTN_DH = D_MODEL // XATTN_HEADS
D_FF = 5632
ALPHA = (2.0 * DEPTH) ** 0.25
LN_EPS = 1e-5
NEG_BIG = -1e30
LOG2E = math.log2(math.e)
LANES = 128
SUBLANES = 8

VMEM_LIMIT_BYTES = 56 * 1024 * 1024

_NT = (((1,), (1,)), ((), ()))
_TN = (((0,), (0,)), ((), ()))


def _params(*sem):
    return pltpu.CompilerParams(dimension_semantics=sem, vmem_limit_bytes=VMEM_LIMIT_BYTES)


def _dot(a, b):
    return jnp.dot(a, b, preferred_element_type=F32)


def _layer_norm(z, g, b):
    mu = jnp.mean(z, axis=-1, keepdims=True)
    d = z - mu
    var = jnp.mean(d * d, axis=-1, keepdims=True)
    return d * lax.rsqrt(var + LN_EPS) * g + b


def _silu(x):
    return x * jax.nn.sigmoid(x)


def _mm_kernel(x_ref, w_ref, *rest, scaled):
    o_ref, wb_ref = rest[-2:]

    @pl.when(pl.program_id(1) == 0)
    def _():
        w = w_ref[...]
        if scaled:
            w = w * rest[0][...]
        wb_ref[...] = w.astype(BF16)

    o_ref[...] = _dot(x_ref[...], wb_ref[...]).astype(o_ref.dtype)


def _matmul(x, w3, lead, n_blocks, out_dtype, src_block=lambda j: j, col_scale=None, tm=1024, tn=1024):
    M, K = x.shape
    tm = min(tm, M)
    scaled = col_scale is not None
    in_specs = [pl.BlockSpec((tm, K), lambda j, i: (i, 0)),
                pl.BlockSpec((None, K, tn), lambda j, i: (lead, 0, src_block(j)))]
    args = [x, w3]
    if scaled:
        in_specs.append(pl.BlockSpec((1, tn), lambda j, i: (0, src_block(j))))
        args.append(col_scale)
    return pl.pallas_call(
        functools.partial(_mm_kernel, scaled=scaled),
        grid=(n_blocks, M // tm),
        in_specs=in_specs,
        out_specs=pl.BlockSpec((tm, tn), lambda j, i: (i, j)),
        out_shape=jax.ShapeDtypeStruct((M, n_blocks * tn), out_dtype),
        scratch_shapes=[pltpu.VMEM((K, tn), BF16)],
        compiler_params=_params("parallel", "arbitrary"),
    )(*args)


def _mm_glu_kernel(x_ref, wa_ref, wb_ref, ba_ref, bb_ref, o_ref):
    x = x_ref[...]
    a = _dot(x, wa_ref[...]) + ba_ref[...]
    b = _dot(x, wb_ref[...]) + bb_ref[...]
    o_ref[...] = a * jax.nn.sigmoid(b)


def _matmul_glu(x, w, bias, tm=1024, tn=512):
    M, K = x.shape
    N = w.shape[1] // 2
    nb = N // tn
    bias2 = bias.reshape(1, 2 * N)
    return pl.pallas_call(
        _mm_glu_kernel,
        grid=(M // tm, nb),
        in_specs=[pl.BlockSpec((tm, K), lambda i, j: (i, 0)),
                  pl.BlockSpec((K, tn), lambda i, j: (0, j)),
                  pl.BlockSpec((K, tn), lambda i, j: (0, j + nb)),
                  pl.BlockSpec((1, tn), lambda i, j: (0, j)),
                  pl.BlockSpec((1, tn), lambda i, j: (0, j + nb))],
        out_specs=pl.BlockSpec((tm, tn), lambda i, j: (i, j)),
        out_shape=jax.ShapeDtypeStruct((M, N), F32),
        compiler_params=_params("parallel", "arbitrary"),
    )(x, w, w, bias2, bias2)


def _mm_ln_kernel(*refs, n_pairs, emit_bf16):
    xs = refs[:n_pairs]
    ws = refs[n_pairs:2 * n_pairs]
    bias_ref, r_ref, g_ref, b_ref = refs[2 * n_pairs:2 * n_pairs + 4]
    outs = refs[2 * n_pairs + 4:]
    acc = _dot(xs[0][...], ws[0][...])
    for x_ref, w_ref in zip(xs[1:], ws[1:]):
        acc = acc + _dot(x_ref[...], w_ref[...])
    z = ALPHA * r_ref[...] + acc + bias_ref[...]
    y = _layer_norm(z, g_ref[...], b_ref[...])
    outs[0][...] = y
    if emit_bf16:
        outs[1][...] = y.astype(BF16)


def _matmul_ln(xs, ws, bias, resid, g, b, emit_bf16=True, tm=512):
    M, N = resid.shape
    n = len(xs)
    row = lambda i: (i, 0)
    fixed = lambda i: (0, 0)
    in_specs = ([pl.BlockSpec((tm, x.shape[1]), row) for x in xs]
                + [pl.BlockSpec(w.shape, fixed) for w in ws]
                + [pl.BlockSpec((1, N), fixed), pl.BlockSpec((tm, N), row),
                   pl.BlockSpec((1, N), fixed), pl.BlockSpec((1, N), fixed)])
    out_specs = [pl.BlockSpec((tm, N), row)]
    out_shape = [jax.ShapeDtypeStruct((M, N), F32)]
    if emit_bf16:
        out_specs.append(pl.BlockSpec((tm, N), row))
        out_shape.append(jax.ShapeDtypeStruct((M, N), BF16))
    outs = pl.pallas_call(
        functools.partial(_mm_ln_kernel, n_pairs=n, emit_bf16=emit_bf16),
        grid=(M // tm,),
        in_specs=in_specs, out_specs=out_specs, out_shape=out_shape,
        compiler_params=_params("parallel"),
    )(*xs, *ws, bias.reshape(1, N), resid, g.reshape(1, N), b.reshape(1, N))
    return outs if emit_bf16 else (outs[0], None)


def _ffn_kernel(x_ref, wg_ref, wu_ref, wo_ref, r_ref, g_ref, b_ref, *rest, nf, emit_bf16):
    outs, acc_ref = rest[:-1], rest[-1]
    f = pl.program_id(1)
    x = x_ref[...]
    hg = _dot(x, wg_ref[...])
    hu = _dot(x, wu_ref[...])
    a = (_silu(hg) * hu).astype(BF16)
    contrib = _dot(a, wo_ref[...])

    @pl.when(f == 0)
    def _():
        acc_ref[...] = contrib

    @pl.when(f > 0)
    def _():
        acc_ref[...] += contrib

    @pl.when(f == nf - 1)
    def _():
        z = ALPHA * r_ref[...] + acc_ref[...]
        y = _layer_norm(z, g_ref[...], b_ref[...])
        outs[0][...] = y
        if emit_bf16:
            outs[1][...] = y.astype(BF16)


def _ffn(xb, w_in, w_out, resid, g, b, emit_bf16=True, tm=512, tf=512):
    M, D = resid.shape
    nf = D_FF // tf
    row = lambda i, f: (i, 0)
    fixed = lambda i, f: (0, 0)
    out_specs = [pl.BlockSpec((tm, D), row)]
    out_shape = [jax.ShapeDtypeStruct((M, D), F32)]
    if emit_bf16:
        out_specs.append(pl.BlockSpec((tm, D), row))
        out_shape.append(jax.ShapeDtypeStruct((M, D), BF16))
    outs = pl.pallas_call(
        functools.partial(_ffn_kernel, nf=nf, emit_bf16=emit_bf16),
        grid=(M // tm, nf),
        in_specs=[pl.BlockSpec((tm, D), row),
                  pl.BlockSpec((D, tf), lambda i, f: (0, f)),
                  pl.BlockSpec((D, tf), lambda i, f: (0, f + nf)),
                  pl.BlockSpec((tf, D), lambda i, f: (f, 0)),
                  pl.BlockSpec((tm, D), row),
                  pl.BlockSpec((1, D), fixed), pl.BlockSpec((1, D), fixed)],
        out_specs=out_specs, out_shape=out_shape,
        scratch_shapes=[pltpu.VMEM((tm, D), F32)],
        compiler_params=_params("parallel", "arbitrary"),
    )(xb, w_in, w_in, w_out, resid, g.reshape(1, D), b.reshape(1, D))
    return outs if emit_bf16 else (outs[0], None)


def _ret_kernel(q_ref, k_ref, v_ref, gate_ref, gg_ref, gb_ref, o_ref, state_ref, *, nch):
    C = RET_CHUNK

    @pl.when(pl.program_id(0) == 0)
    def _():
        state_ref[...] = jnp.zeros_like(state_ref)

    scale = RET_DK ** -0.5
    row = lax.broadcasted_iota(jnp.int32, (C, C), 0).astype(F32)
    col = lax.broadcasted_iota(jnp.int32, (C, C), 1).astype(F32)
    dist = row - col
    for h in range(RET_HEADS):
        log_gamma = math.log1p(-(2.0 ** (-5.0 - h)))
        decay = jnp.where(dist >= 0, jnp.exp(log_gamma * jnp.maximum(dist, 0.0)), 0.0) * scale
        xi = jnp.exp(log_gamma * (row + 1.0))
        zeta = jnp.exp(log_gamma * (C - 1.0 - row)) * scale
        chunk_decay = math.exp(log_gamma * C)
        cs = slice(h * RET_DK, (h + 1) * RET_DK)
        gg = gg_ref[:, cs]
        gb = gb_ref[:, cs]
        for c in range(nch):
            rs = slice(c * C, (c + 1) * C)
            q = q_ref[rs, cs]
            k = k_ref[rs, cs]
            v = v_ref[rs, cs]
            state = state_ref[h]
            scores = lax.dot_general(q, k, _NT, preferred_element_type=F32) * decay
            y = _dot(scores.astype(BF16), v) + xi * _dot(q, state.astype(BF16))
            kz = (k.astype(F32) * zeta).astype(BF16)
            kv = lax.dot_general(kz, v, _TN, preferred_element_type=F32)
            state_ref[h] = state * chunk_decay + kv
            mu = jnp.mean(y, axis=-1, keepdims=True)
            d = y - mu
            var = jnp.mean(d * d, axis=-1, keepdims=True)
            yn = d * lax.rsqrt(var + LN_EPS) * gg + gb
            o_ref[rs, cs] = (_silu(gate_ref[rs, cs]) * yn).astype(BF16)


def _retention(qkv, gate, gn_g, gn_b, tr=512):
    S = qkv.shape[0]
    tr = min(tr, S)
    fixed = lambda i: (0, 0)
    return pl.pallas_call(
        functools.partial(_ret_kernel, nch=tr // RET_CHUNK),
        grid=(S // tr,),
        in_specs=[pl.BlockSpec((tr, RET_W), lambda i: (i, 0)),
                  pl.BlockSpec((tr, RET_W), lambda i: (i, 1)),
                  pl.BlockSpec((tr, RET_W), lambda i: (i, 2)),
                  pl.BlockSpec((tr, RET_W), lambda i: (i, 0)),
                  pl.BlockSpec((1, RET_W), fixed), pl.BlockSpec((1, RET_W), fixed)],
        out_specs=pl.BlockSpec((tr, RET_W), lambda i: (i, 0)),
        out_shape=jax.ShapeDtypeStruct((S, RET_W), BF16),
        scratch_shapes=[pltpu.VMEM((RET_HEADS, RET_DK, RET_DK), F32)],
        compiler_params=_params("arbitrary"),
    )(qkv, qkv, qkv, gate, gn_g.reshape(1, RET_W), gn_b.reshape(1, RET_W))


def _diff_kernel(lam_ref, q_ref, k_ref, v_ref, g_ref, o_ref, qa_ref, kb_ref, s0_ref, s1_ref, p_ref, alpha_ref,
                 m_ref, l_ref, acc_ref, *, tq, lambda_init):
    h = pl.program_id(0)
    i = pl.program_id(1)
    tk = tq
    m_ref[...] = jnp.full(m_ref.shape, NEG_BIG, F32)
    l_ref[...] = jnp.zeros_like(l_ref)
    acc_ref[...] = jnp.zeros_like(acc_ref)

    head = (h + 1).astype(F32)
    slope = jnp.exp2(jnp.full((tk, LANES), -8.0 / DIFF_HEADS, F32) * head) * LOG2E
    lane = lax.broadcasted_iota(jnp.int32, (tk, LANES), 1)
    kbias = slope * lax.broadcasted_iota(jnp.int32, (tk, LANES), 0).astype(F32)
    hi = kbias.astype(BF16).astype(F32)
    mid = (kbias - hi).astype(BF16).astype(F32)
    lo = kbias - hi - mid
    kb_ref[...] = jnp.where(lane == 0, hi, jnp.where(lane == 1, mid, jnp.where(lane == 2, lo, 0.0))).astype(BF16)
    ones = jnp.where(lane < 3, 1.0, 0.0).astype(BF16)
    for mi in range(2):
        qa_ref[mi, :, :DIFF_DK] = q_ref[:, mi * DIFF_DK:(mi + 1) * DIFF_DK]
        qa_ref[mi, :, DIFF_DK:] = ones

    def scores(j, s_ref):
        ks = k_ref[pl.ds(pl.multiple_of(j * tk, tk), tk), :]
        for mi in range(2):
            ka = jnp.concatenate([ks[:, mi * DIFF_DK:(mi + 1) * DIFF_DK], kb_ref[...]], axis=1)
            s_ref[mi] = lax.dot_general(qa_ref[mi], ka, _NT, preferred_element_type=F32)

    def update(j, s_ref, masked):
        vs = v_ref[pl.ds(pl.multiple_of(j * tk, tk), tk), :]
        tile_bias = slope[:1, :] * ((j - i) * tk).astype(F32)
        for mi in range(2):
            for r0 in range(0, tq, DIFF_ROW_CHUNK):
                rs = slice(r0, r0 + DIFF_ROW_CHUNK)
                s = s_ref[mi, rs, :]
                if masked:
                    r = lax.broadcasted_iota(jnp.int32, (DIFF_ROW_CHUNK, tk), 0) + r0
                    c = lax.broadcasted_iota(jnp.int32, (DIFF_ROW_CHUNK, tk), 1)
                    s = jnp.where(r >= c, s, NEG_BIG)
                m_old = m_ref[mi, rs, :]
                m_new = jnp.maximum(m_old, jnp.max(s, axis=-1, keepdims=True) + tile_bias)
                p = jnp.exp2(s - jnp.tile(m_new - tile_bias, (1, tk // LANES)))
                alpha = jnp.exp2(m_old - m_new)
                psum = p[:, :LANES]
                for t in range(1, tk // LANES):
                    psum = psum + p[:, t * LANES:(t + 1) * LANES]
                l_ref[mi, rs, :] = alpha * l_ref[mi, rs, :] + psum
                m_ref[mi, rs, :] = m_new
                alpha_ref[mi, rs, :] = alpha
                p_ref[mi, rs, :] = p.astype(BF16)
            acc_ref[mi] = (jnp.tile(alpha_ref[mi], (1, DIFF_DV // LANES)) * acc_ref[mi]
                           + _dot(p_ref[mi], vs))

    scores(0, s0_ref)

    def pair(t, carry):
        j = 2 * t
        scores(j + 1, s1_ref)
        update(j, s0_ref, masked=False)
        scores(j + 2, s0_ref)
        update(j + 1, s1_ref, masked=False)
        return carry

    lax.fori_loop(0, i // 2, pair, 0)

    @pl.when(i % 2 == 1)
    def _():
        scores(i, s1_ref)
        update(i - 1, s0_ref, masked=False)
        update(i, s1_ref, masked=True)

    @pl.when(i % 2 == 0)
    def _():
        update(i, s0_ref, masked=True)

    lf = lam_ref[...]
    lam = (jnp.exp(jnp.sum(lf[0:1] * lf[1:2], axis=-1, keepdims=True))
           - jnp.exp(jnp.sum(lf[2:3] * lf[3:4], axis=-1, keepdims=True)) + lambda_init)
    inv1 = 1.0 / jnp.sum(l_ref[0], axis=-1, keepdims=True)
    inv2 = 1.0 / jnp.sum(l_ref[1], axis=-1, keepdims=True)
    o = acc_ref[0] * inv1 - lam * (acc_ref[1] * inv2)
    y = o * lax.rsqrt(jnp.mean(o * o, axis=-1, keepdims=True) + LN_EPS)
    o_ref[...] = (y * g_ref[...] * (1.0 - lambda_init)).astype(BF16)


def _diff_attention(qkv, diff_lambda, subln_g, lambda_init, tq=512):
    S = qkv.shape[0]
    tq = min(tq, S)
    qoff = 3 * RET_W // DIFF_DV
    koff = qoff + DIFF_W // DIFF_DV
    voff = koff + DIFF_W // DIFF_DV
    return pl.pallas_call(
        functools.partial(_diff_kernel, tq=tq, lambda_init=lambda_init),
        grid=(DIFF_HEADS, S // tq),
        in_specs=[pl.BlockSpec((4, DIFF_DK), lambda h, i: (0, 0)),
                  pl.BlockSpec((tq, DIFF_DV), lambda h, i: (i, qoff + h)),
                  pl.BlockSpec((S, DIFF_DV), lambda h, i: (0, koff + h)),
                  pl.BlockSpec((S, DIFF_DV), lambda h, i: (0, voff + h)),
                  pl.BlockSpec((1, DIFF_DV), lambda h, i: (0, h))],
        out_specs=pl.BlockSpec((tq, DIFF_DV), lambda h, i: (i, h)),
        out_shape=jax.ShapeDtypeStruct((S, DIFF_W), BF16),
        scratch_shapes=[pltpu.VMEM((2, tq, 2 * DIFF_DK), BF16), pltpu.VMEM((tq, LANES), BF16),
                        pltpu.VMEM((2, tq, tq), F32), pltpu.VMEM((2, tq, tq), F32),
                        pltpu.VMEM((2, tq, tq), BF16), pltpu.VMEM((2, tq, LANES), F32),
                        pltpu.VMEM((2, tq, LANES), F32), pltpu.VMEM((2, tq, LANES), F32),
                        pltpu.VMEM((2, tq, DIFF_DV), F32)],
        compiler_params=_params("parallel", "arbitrary"),
    )(diff_lambda, qkv, qkv, qkv, subln_g.reshape(1, DIFF_W))


def _conv_kernel(cur_ref, halo_ref, w_ref, bdw_ref, g_ref, b_ref, o_ref, ext_ref, sh_ref, y_ref, *, tm):
    i = pl.program_id(0)
    ext_ref[0:CONV_HALO, :] = jnp.where(i > 0, halo_ref[...], 0.0)
    ext_ref[CONV_HALO:, :] = cur_ref[...]
    first = CONV_HALO - (CONV_WIDTH - 1)
    D = cur_ref.shape[1]
    nc = D // LANES
    for b in range(SUBLANES):
        n = tm + (CONV_WIDTH - 1 - b) // SUBLANES * SUBLANES
        for c in range(nc):
            sh_ref[b, c, 0:n, :] = ext_ref[first + b:first + b + n, c * LANES:(c + 1) * LANES]

    def channel_block(c, carry):
        acc = jnp.broadcast_to(bdw_ref[c], (tm, LANES))
        for j in range(CONV_WIDTH):
            b = j % SUBLANES
            acc = acc + w_ref[j, c] * sh_ref[b, c, j - b:j - b + tm, :]
        y_ref[c] = acc
        return carry

    lax.fori_loop(0, nc, channel_block, 0)

    total = y_ref[0]
    for c in range(1, nc):
        total = total + y_ref[c]
    mu = jnp.sum(total, axis=-1, keepdims=True) * (1.0 / D)
    sq = jnp.square(y_ref[0] - mu)
    for c in range(1, nc):
        sq = sq + jnp.square(y_ref[c] - mu)
    rstd = lax.rsqrt(jnp.sum(sq, axis=-1, keepdims=True) * (1.0 / D) + LN_EPS)
    for c in range(nc):
        cs = slice(c * LANES, (c + 1) * LANES)
        y = (y_ref[c] - mu) * rstd * g_ref[:, cs] + b_ref[:, cs]
        o_ref[:, cs] = _silu(y).astype(BF16)


def _conv_ln_silu(u, w_dw, b_dw, g, b, tm=128):
    S, D = u.shape
    per = tm // CONV_HALO
    nc = D // LANES
    fixed = lambda i: (0, 0)
    return pl.pallas_call(
        functools.partial(_conv_kernel, tm=tm),
        grid=(S // tm,),
        in_specs=[pl.BlockSpec((tm, D), lambda i: (i, 0)),
                  pl.BlockSpec((CONV_HALO, D), lambda i: (jnp.maximum(i * per - 1, 0), 0)),
                  pl.BlockSpec((CONV_WIDTH, nc, 1, LANES), lambda i: (0, 0, 0, 0)),
                  pl.BlockSpec((nc, 1, LANES), lambda i: (0, 0, 0)),
                  pl.BlockSpec((1, D), fixed), pl.BlockSpec((1, D), fixed)],
        out_specs=pl.BlockSpec((tm, D), lambda i: (i, 0)),
        out_shape=jax.ShapeDtypeStruct((S, D), BF16),
        scratch_shapes=[pltpu.VMEM((tm + CONV_HALO, D), F32),
                        pltpu.VMEM((SUBLANES, nc, tm + CONV_HALO - SUBLANES, LANES), F32),
                        pltpu.VMEM((nc, tm, LANES), F32)],
        compiler_params=_params("parallel"),
    )(u, u, w_dw.reshape(CONV_WIDTH, nc, 1, LANES), b_dw.reshape(nc, 1, LANES), g.reshape(1, D), b.reshape(1, D))


def _xattn_kernel(q_ref, k_ref, v_ref, wo_ref, r_ref, g_ref, b_ref, of_ref, ob_ref):
    scale = XATTN_DH ** -0.5
    heads = []
    for h in range(XATTN_HEADS):
        cs = slice(h * XATTN_DH, (h + 1) * XATTN_DH)
        s = lax.dot_general(q_ref[:, cs], k_ref[:, cs], _NT, preferred_element_type=F32) * scale
        m = jnp.max(s, axis=-1, keepdims=True)
        p = jnp.exp(s - m)
        p = p * (1.0 / jnp.sum(p, axis=-1, keepdims=True))
        heads.append(_dot(p.astype(BF16), v_ref[:, cs]).astype(BF16))
    o = jnp.concatenate(heads, axis=-1)
    z = ALPHA * r_ref[...] + _dot(o, wo_ref[...])
    y = _layer_norm(z, g_ref[...], b_ref[...])
    of_ref[...] = y
    ob_ref[...] = y.astype(BF16)


def _xattn(q, k, v, wo, resid, g, b, tm=512):
    M, D = resid.shape
    row = lambda i: (i, 0)
    fixed = lambda i: (0, 0)
    return pl.pallas_call(
        _xattn_kernel,
        grid=(M // tm,),
        in_specs=[pl.BlockSpec((tm, D), row),
                  pl.BlockSpec(k.shape, fixed), pl.BlockSpec(v.shape, fixed),
                  pl.BlockSpec((D, D), fixed),
                  pl.BlockSpec((tm, D), row),
                  pl.BlockSpec((1, D), fixed), pl.BlockSpec((1, D), fixed)],
        out_specs=[pl.BlockSpec((tm, D), row), pl.BlockSpec((tm, D), row)],
        out_shape=[jax.ShapeDtypeStruct((M, D), F32), jax.ShapeDtypeStruct((M, D), BF16)],
        compiler_params=_params("parallel"),
    )(q, k, v, wo, resid, g.reshape(1, D), b.reshape(1, D))


def _lambda_init_for(layer):
    return 0.8 - 0.6 * math.exp(-0.3 * layer)


def kernel(x, mem, w_in, ret_gn_g, ret_gn_b, diff_lambda, diff_subln_g, w_mix_out, conv_w_pw1, conv_b_pw1, conv_w_dw, conv_b_dw, conv_ln_g, conv_ln_b, conv_w_pw2, conv_b_pw2, xattn_wq, xattn_wk, xattn_wv, xattn_wo, ffn_w_in, ffn_w_out, ln_g, ln_b):
    B, S, D = x.shape
    assert B == 1 and D == D_MODEL
    h = x[0]
    hb = h.astype(BF16)
    memb = mem[0].astype(BF16)
    zero_bias = jnp.zeros((D,), F32)
    for layer in range(DEPTH):
        last = layer == DEPTH - 1
        if layer % 2 == 0:
            e = layer // 2
            gate_blk, dq_blk = 3, 4
            col_scale = jnp.ones((1, w_in.shape[2]), F32).at[:, dq_blk * DIFF_W:(dq_blk + 1) * DIFF_W].set(
                DIFF_DK ** -0.5 * LOG2E)
            qkv = _matmul(hb, w_in, e, 6, BF16, src_block=lambda j: jnp.where(j >= gate_blk, j + 1, j),
                          col_scale=col_scale)
            gate = _matmul(hb, w_in, e, 1, F32, src_block=lambda j: j + gate_blk)
            ret = _retention(qkv, gate, ret_gn_g[e], ret_gn_b[e])
            dif = _diff_attention(qkv, diff_lambda[e], diff_subln_g[e], _lambda_init_for(layer + 1))
            wo = w_mix_out[e].astype(BF16)
            h, hb = _matmul_ln([ret, dif], [wo[:RET_W], wo[RET_W:]], zero_bias, h,
                               ln_g[layer, 0], ln_b[layer, 0])
        else:
            o = layer // 2
            u = _matmul_glu(hb, conv_w_pw1[o].astype(BF16), conv_b_pw1[o])
            c = _conv_ln_silu(u, conv_w_dw[o], conv_b_dw[o], conv_ln_g[o], conv_ln_b[o])
            h, hb = _matmul_ln([c], [conv_w_pw2[o].astype(BF16)], conv_b_pw2[o], h,
                               ln_g[layer, 0], ln_b[layer, 0])
        q = _matmul(hb, xattn_wq, layer, 2, BF16)
        k = _matmul(memb, xattn_wk, layer, 2, BF16)
        v = _matmul(memb, xattn_wv, layer, 2, BF16)
        h, hb = _xattn(q, k, v, xattn_wo[layer].astype(BF16), h, ln_g[layer, 1], ln_b[layer, 1])
        h, hb = _ffn(hb, ffn_w_in[layer].astype(BF16), ffn_w_out[layer].astype(BF16), h,
                     ln_g[layer, 2], ln_b[layer, 2], emit_bf16=not last)
    return h[None]
```

```python
import functools
import math

import jax
import jax.numpy as jnp
from jax import lax
from jax.experimental import pallas as pl
from jax.experimental.pallas import tpu as pltpu

F32 = jnp.float32
BF16 = jnp.bfloat16

D_MODEL = 2048
DEPTH = 2
RET_HEADS = 8
RET_DK = 128
RET_CHUNK = 128
RET_W = 1024
DIFF_HEADS = 4
DIFF_DK = 128
DIFF_DV = 256
DIFF_W = 1024
DIFF_ROW_CHUNK = 32
DIFF_KEY_TILES = 2
CONV_WIDTH = 31
CONV_HALO = 32
XATTN_HEADS = 4
XATTN_DH = D_MODEL // XATTN_HEADS
D_FF = 5632
ALPHA = (2.0 * DEPTH) ** 0.25
LN_EPS = 1e-5
NEG_BIG = -1e30
LOG2E = math.log2(math.e)
LANES = 128
SUBLANES = 8

VMEM_LIMIT_BYTES = 56 * 1024 * 1024

_NT = (((1,), (1,)), ((), ()))
_TN = (((0,), (0,)), ((), ()))


def _params(*sem):
    return pltpu.CompilerParams(dimension_semantics=sem, vmem_limit_bytes=VMEM_LIMIT_BYTES)


def _dot(a, b):
    return jnp.dot(a, b, preferred_element_type=F32)


def _layer_norm(z, g, b):
    mu = jnp.mean(z, axis=-1, keepdims=True)
    d = z - mu
    var = jnp.mean(d * d, axis=-1, keepdims=True)
    return d * lax.rsqrt(var + LN_EPS) * g + b


def _silu(x):
    return x * jax.nn.sigmoid(x)


def _mm_kernel(x_ref, w_ref, *rest, scaled):
    o_ref, wb_ref = rest[-2:]

    @pl.when(pl.program_id(1) == 0)
    def _():
        w = w_ref[...]
        if scaled:
            w = w * rest[0][...]
        wb_ref[...] = w.astype(BF16)

    o_ref[...] = _dot(x_ref[...], wb_ref[...]).astype(o_ref.dtype)


def _matmul(x, w3, lead, n_blocks, out_dtype, src_block=lambda j: j, col_scale=None, tm=1024, tn=1024):
    M, K = x.shape
    tm = min(tm, M)
    scaled = col_scale is not None
    in_specs = [pl.BlockSpec((tm, K), lambda j, i: (i, 0)),
                pl.BlockSpec((None, K, tn), lambda j, i: (lead, 0, src_block(j)))]
    args = [x, w3]
    if scaled:
        in_specs.append(pl.BlockSpec((1, tn), lambda j, i: (0, src_block(j))))
        args.append(col_scale)
    return pl.pallas_call(
        functools.partial(_mm_kernel, scaled=scaled),
        grid=(n_blocks, M // tm),
        in_specs=in_specs,
        out_specs=pl.BlockSpec((tm, tn), lambda j, i: (i, j)),
        out_shape=jax.ShapeDtypeStruct((M, n_blocks * tn), out_dtype),
        scratch_shapes=[pltpu.VMEM((K, tn), BF16)],
        compiler_params=_params("parallel", "arbitrary"),
    )(*args)


def _mm_glu_kernel(x_ref, wa_ref, wb_ref, ba_ref, bb_ref, o_ref):
    x = x_ref[...]
    a = _dot(x, wa_ref[...]) + ba_ref[...]
    b = _dot(x, wb_ref[...]) + bb_ref[...]
    o_ref[...] = a * jax.nn.sigmoid(b)


def _matmul_glu(x, w, bias, tm=1024, tn=512):
    M, K = x.shape
    N = w.shape[1] // 2
    nb = N // tn
    bias2 = bias.reshape(1, 2 * N)
    return pl.pallas_call(
        _mm_glu_kernel,
        grid=(M // tm, nb),
        in_specs=[pl.BlockSpec((tm, K), lambda i, j: (i, 0)),
                  pl.BlockSpec((K, tn), lambda i, j: (0, j)),
                  pl.BlockSpec((K, tn), lambda i, j: (0, j + nb)),
                  pl.BlockSpec((1, tn), lambda i, j: (0, j)),
                  pl.BlockSpec((1, tn), lambda i, j: (0, j + nb))],
        out_specs=pl.BlockSpec((tm, tn), lambda i, j: (i, j)),
        out_shape=jax.ShapeDtypeStruct((M, N), F32),
        compiler_params=_params("parallel", "arbitrary"),
    )(x, w, w, bias2, bias2)


def _mm_ln_kernel(*refs, n_pairs, emit_bf16):
    xs = refs[:n_pairs]
    ws = refs[n_pairs:2 * n_pairs]
    bias_ref, r_ref, g_ref, b_ref = refs[2 * n_pairs:2 * n_pairs + 4]
    outs = refs[2 * n_pairs + 4:]
    acc = _dot(xs[0][...], ws[0][...])
    for x_ref, w_ref in zip(xs[1:], ws[1:]):
        acc = acc + _dot(x_ref[...], w_ref[...])
    z = ALPHA * r_ref[...] + acc + bias_ref[...]
    y = _layer_norm(z, g_ref[...], b_ref[...])
    outs[0][...] = y
    if emit_bf16:
        outs[1][...] = y.astype(BF16)


def _matmul_ln(xs, ws, bias, resid, g, b, emit_bf16=True, tm=512):
    M, N = resid.shape
    n = len(xs)
    row = lambda i: (i, 0)
    fixed = lambda i: (0, 0)
    in_specs = ([pl.BlockSpec((tm, x.shape[1]), row) for x in xs]
                + [pl.BlockSpec(w.shape, fixed) for w in ws]
                + [pl.BlockSpec((1, N), fixed), pl.BlockSpec((tm, N), row),
                   pl.BlockSpec((1, N), fixed), pl.BlockSpec((1, N), fixed)])
    out_specs = [pl.BlockSpec((tm, N), row)]
    out_shape = [jax.ShapeDtypeStruct((M, N), F32)]
    if emit_bf16:
        out_specs.append(pl.BlockSpec((tm, N), row))
        out_shape.append(jax.ShapeDtypeStruct((M, N), BF16))
    outs = pl.pallas_call(
        functools.partial(_mm_ln_kernel, n_pairs=n, emit_bf16=emit_bf16),
        grid=(M // tm,),
        in_specs=in_specs, out_specs=out_specs, out_shape=out_shape,
        compiler_params=_params("parallel"),
    )(*xs, *ws, bias.reshape(1, N), resid, g.reshape(1, N), b.reshape(1, N))
    return outs if emit_bf16 else (outs[0], None)


def _ffn_kernel(x_ref, wg_ref, wu_ref, wo_ref, r_ref, g_ref, b_ref, *rest, nf, emit_bf16):
    outs, a_ref, acc_ref = rest[:-2], rest[-2], rest[-1]
    f = pl.program_id(1)

    def up():
        x = x_ref[...]
        a_ref[...] = (_silu(_dot(x, wg_ref[...])) * _dot(x, wu_ref[...])).astype(BF16)

    def down():
        acc_ref[...] += _dot(a_ref[...], wo_ref[...])

    @pl.when(f == 0)
    def _():
        acc_ref[...] = jnp.zeros_like(acc_ref)
        up()

    @pl.when(jnp.logical_and(f > 0, f < nf))
    def _():
        down()
        up()

    @pl.when(f == nf)
    def _():
        down()
        z = ALPHA * r_ref[...] + acc_ref[...]
        y = _layer_norm(z, g_ref[...], b_ref[...])
        outs[0][...] = y
        if emit_bf16:
            outs[1][...] = y.astype(BF16)


def _ffn(xb, w_in, w_out, resid, g, b, emit_bf16=True, tm=512, tf=512):
    M, D = resid.shape
    nf = D_FF // tf
    row = lambda i, f: (i, 0)
    fixed = lambda i, f: (0, 0)
    out_specs = [pl.BlockSpec((tm, D), row)]
    out_shape = [jax.ShapeDtypeStruct((M, D), F32)]
    if emit_bf16:
        out_specs.append(pl.BlockSpec((tm, D), row))
        out_shape.append(jax.ShapeDtypeStruct((M, D), BF16))
    outs = pl.pallas_call(
        functools.partial(_ffn_kernel, nf=nf, emit_bf16=emit_bf16),
        grid=(M // tm, nf + 1),
        in_specs=[pl.BlockSpec((tm, D), row),
                  pl.BlockSpec((D, tf), lambda i, f: (0, jnp.minimum(f, nf - 1))),
                  pl.BlockSpec((D, tf), lambda i, f: (0, jnp.minimum(f, nf - 1) + nf)),
                  pl.BlockSpec((tf, D), lambda i, f: (jnp.maximum(f - 1, 0), 0)),
                  pl.BlockSpec((tm, D), row),
                  pl.BlockSpec((1, D), fixed), pl.BlockSpec((1, D), fixed)],
        out_specs=out_specs, out_shape=out_shape,
        scratch_shapes=[pltpu.VMEM((tm, tf), BF16), pltpu.VMEM((tm, D), F32)],
        compiler_params=_params("parallel", "arbitrary"),
    )(xb, w_in, w_in, w_out, resid, g.reshape(1, D), b.reshape(1, D))
    return outs if emit_bf16 else (outs[0], None)


def _ret_kernel(q_ref, k_ref, v_ref, gate_ref, gg_ref, gb_ref, o_ref, state_ref, *, nch):
    C = RET_CHUNK

    @pl.when(pl.program_id(0) == 0)
    def _():
        state_ref[...] = jnp.zeros_like(state_ref)

    scale = RET_DK ** -0.5
    row = lax.broadcasted_iota(jnp.int32, (C, C), 0).astype(F32)
    col = lax.broadcasted_iota(jnp.int32, (C, C), 1).astype(F32)
    dist = row - col
    for h in range(RET_HEADS):
        log_gamma = math.log1p(-(2.0 ** (-5.0 - h)))
        decay = jnp.where(dist >= 0, jnp.exp(log_gamma * jnp.maximum(dist, 0.0)), 0.0) * scale
        xi = jnp.exp(log_gamma * (row + 1.0))
        zeta = jnp.exp(log_gamma * (C - 1.0 - row)) * scale
        chunk_decay = math.exp(log_gamma * C)
        cs = slice(h * RET_DK, (h + 1) * RET_DK)
        gg = gg_ref[:, cs]
        gb = gb_ref[:, cs]
        for c in range(nch):
            rs = slice(c * C, (c + 1) * C)
            q = q_ref[rs, cs]
            k = k_ref[rs, cs]
            v = v_ref[rs, cs]
            state = state_ref[h]
            scores = lax.dot_general(q, k, _NT, preferred_element_type=F32) * decay
            y = _dot(scores.astype(BF16), v) + xi * _dot(q, state.astype(BF16))
            kz = (k.astype(F32) * zeta).astype(BF16)
            kv = lax.dot_general(kz, v, _TN, preferred_element_type=F32)
            state_ref[h] = state * chunk_decay + kv
            mu = jnp.mean(y, axis=-1, keepdims=True)
            d = y - mu
            var = jnp.mean(d * d, axis=-1, keepdims=True)
            yn = d * lax.rsqrt(var + LN_EPS) * gg + gb
            o_ref[rs, cs] = (_silu(gate_ref[rs, cs]) * yn).astype(BF16)


def _retention(qkv, gate, gn_g, gn_b, tr=512):
    S = qkv.shape[0]
    tr = min(tr, S)
    fixed = lambda i: (0, 0)
    return pl.pallas_call(
        functools.partial(_ret_kernel, nch=tr // RET_CHUNK),
        grid=(S // tr,),
        in_specs=[pl.BlockSpec((tr, RET_W), lambda i: (i, 0)),
                  pl.BlockSpec((tr, RET_W), lambda i: (i, 1)),
                  pl.BlockSpec((tr, RET_W), lambda i: (i, 2)),
                  pl.BlockSpec((tr, RET_W), lambda i: (i, 0)),
                  pl.BlockSpec((1, RET_W), fixed), pl.BlockSpec((1, RET_W), fixed)],
        out_specs=pl.BlockSpec((tr, RET_W), lambda i: (i, 0)),
        out_shape=jax.ShapeDtypeStruct((S, RET_W), BF16),
        scratch_shapes=[pltpu.VMEM((RET_HEADS, RET_DK, RET_DK), F32)],
        compiler_params=_params("arbitrary"),
    )(qkv, qkv, qkv, gate, gn_g.reshape(1, RET_W), gn_b.reshape(1, RET_W))


def _diff_kernel(lam_ref, q_ref, k_ref, v_ref, g_ref, o_ref, kb_ref, s_ref, p_ref, alpha_ref,
                 m_ref, l_ref, acc_ref, *, tq, lambda_init):
    h = pl.program_id(0)
    i = pl.program_id(1)
    tk = DIFF_KEY_TILES * tq
    n = i // DIFF_KEY_TILES
    m_ref[...] = jnp.full(m_ref.shape, NEG_BIG, F32)
    l_ref[...] = jnp.zeros_like(l_ref)
    acc_ref[...] = jnp.zeros_like(acc_ref)

    head = (h + 1).astype(F32)
    slope = jnp.exp2(jnp.full((tk, LANES), -8.0 / DIFF_HEADS, F32) * head) * LOG2E
    lane = lax.broadcasted_iota(jnp.int32, (tk, LANES), 1)

    def split3(x, first_lane, lanes):
        hi = x.astype(BF16).astype(F32)
        mid = (x - hi).astype(BF16).astype(F32)
        lo = x - hi - mid
        pieces = jnp.where(lanes == first_lane, hi,
                           jnp.where(lanes == first_lane + 1, mid,
                                     jnp.where(lanes == first_lane + 2, lo, 0.0)))
        return pieces.astype(BF16)

    within = slope * lax.broadcasted_iota(jnp.int32, (tk, LANES), 0).astype(F32)
    kb_ref[...] = jnp.where(jnp.logical_and(lane >= 3, lane < 6), 1.0, split3(within, 0, lane).astype(F32)).astype(BF16)
    q_ones = jnp.where(lane[:SUBLANES] < 3, 1.0, 0.0)

    def scores(j, par):
        ks = k_ref[pl.ds(pl.multiple_of(j * tk, tk), tk), :]
        tile_bias = split3(slope[:SUBLANES] * (j * tk - i * tq).astype(F32), 3, lane[:SUBLANES]).astype(F32)
        qb = jnp.tile(q_ones + tile_bias, (tq // SUBLANES, 1)).astype(BF16)
        kb = kb_ref[...]
        for mi in range(2):
            ms = slice(mi * DIFF_DK, (mi + 1) * DIFF_DK)
            qa = jnp.concatenate([q_ref[:, ms], qb], axis=1)
            ka = jnp.concatenate([ks[:, ms], kb], axis=1)
            s_ref[par, mi] = lax.dot_general(qa, ka, _NT, preferred_element_type=F32)

    def softmax(j, par, masked):
        for mi in range(2):
            for r0 in range(0, tq, DIFF_ROW_CHUNK):
                rs = slice(r0, r0 + DIFF_ROW_CHUNK)
                s = s_ref[par, mi, rs, :]
                if masked:
                    r = lax.broadcasted_iota(jnp.int32, (DIFF_ROW_CHUNK, tk), 0) + (r0 + i * tq)
                    c = lax.broadcasted_iota(jnp.int32, (DIFF_ROW_CHUNK, tk), 1) + j * tk
                    s = jnp.where(r >= c, s, NEG_BIG)
                m_old = m_ref[mi, rs, :]
                m_new = jnp.maximum(m_old, jnp.max(s, axis=-1, keepdims=True))
                p = jnp.exp2(s - jnp.tile(m_new, (1, tk // LANES)))
                alpha = jnp.exp2(m_old - m_new)
                psum = p[:, :LANES]
                for t in range(1, tk // LANES):
                    psum = psum + p[:, t * LANES:(t + 1) * LANES]
                l_ref[mi, rs, :] = alpha * l_ref[mi, rs, :] + psum
                m_ref[mi, rs, :] = m_new
                alpha_ref[par, mi, rs, :] = alpha
                p_ref[par, mi, rs, :] = p.astype(BF16)

    def accumulate(j, par):
        vs = v_ref[pl.ds(pl.multiple_of(j * tk, tk), tk), :]
        for mi in range(2):
            acc_ref[mi] = (jnp.tile(alpha_ref[par, mi], (1, DIFF_DV // LANES)) * acc_ref[mi]
                           + _dot(p_ref[par, mi], vs))

    scores(0, 0)

    @pl.when(n == 0)
    def _():
        softmax(0, 0, masked=True)
        accumulate(0, 0)

    @pl.when(n > 0)
    def _():
        scores(1, 1)
        softmax(0, 0, masked=False)

        def pair(t, carry):
            j = 2 * t + 1
            scores(j + 1, 0)
            softmax(j, 1, masked=False)
            accumulate(j - 1, 0)
            scores(j + 2, 1)
            softmax(j + 1, 0, masked=False)
            accumulate(j, 1)
            return carry

        lax.fori_loop(0, (n - 1) // 2, pair, 0)

        @pl.when(n % 2 == 0)
        def _():
            scores(n, 0)
            softmax(n - 1, 1, masked=False)
            accumulate(n - 2, 0)
            softmax(n, 0, masked=True)
            accumulate(n - 1, 1)
            accumulate(n, 0)

        @pl.when(n % 2 == 1)
        def _():
            softmax(n, 1, masked=True)
            accumulate(n - 1, 0)
            accumulate(n, 1)

    lf = lam_ref[...]
    lam = (jnp.exp(jnp.sum(lf[0:1] * lf[1:2], axis=-1, keepdims=True))
           - jnp.exp(jnp.sum(lf[2:3] * lf[3:4], axis=-1, keepdims=True)) + lambda_init)
    inv1 = 1.0 / jnp.sum(l_ref[0], axis=-1, keepdims=True)
    inv2 = 1.0 / jnp.sum(l_ref[1], axis=-1, keepdims=True)
    o = acc_ref[0] * inv1 - lam * (acc_ref[1] * inv2)
    y = o * lax.rsqrt(jnp.mean(o * o, axis=-1, keepdims=True) + LN_EPS)
    o_ref[...] = (y * g_ref[...] * (1.0 - lambda_init)).astype(BF16)


def _diff_attention(qkv, diff_lambda, subln_g, lambda_init, tq=512):
    S = qkv.shape[0]
    tq = min(tq, S // DIFF_KEY_TILES)
    tk = DIFF_KEY_TILES * tq
    qoff = 3 * RET_W // DIFF_DV
    koff = qoff + DIFF_W // DIFF_DV
    voff = koff + DIFF_W // DIFF_DV
    return pl.pallas_call(
        functools.partial(_diff_kernel, tq=tq, lambda_init=lambda_init),
        grid=(DIFF_HEADS, S // tq),
        in_specs=[pl.BlockSpec((4, DIFF_DK), lambda h, i: (0, 0)),
                  pl.BlockSpec((tq, DIFF_DV), lambda h, i: (i, qoff + h)),
                  pl.BlockSpec((S, DIFF_DV), lambda h, i: (0, koff + h), pipeline_mode=pl.Buffered(1)),
                  pl.BlockSpec((S, DIFF_DV), lambda h, i: (0, voff + h), pipeline_mode=pl.Buffered(1)),
                  pl.BlockSpec((1, DIFF_DV), lambda h, i: (0, h))],
        out_specs=pl.BlockSpec((tq, DIFF_DV), lambda h, i: (i, h)),
        out_shape=jax.ShapeDtypeStruct((S, DIFF_W), BF16),
        scratch_shapes=[pltpu.VMEM((tk, LANES), BF16),
                        pltpu.VMEM((2, 2, tq, tk), F32),
                        pltpu.VMEM((2, 2, tq, tk), BF16), pltpu.VMEM((2, 2, tq, LANES), F32),
                        pltpu.VMEM((2, tq, LANES), F32), pltpu.VMEM((2, tq, LANES), F32),
                        pltpu.VMEM((2, tq, DIFF_DV), F32)],
        compiler_params=_params("parallel", "arbitrary"),
    )(diff_lambda, qkv, qkv, qkv, subln_g.reshape(1, DIFF_W))


def _conv_kernel(cur_ref, halo_ref, w_ref, bdw_ref, g_ref, b_ref, o_ref, ext_ref, sh_ref, y_ref, *, tm):
    i = pl.program_id(0)
    ext_ref[0:CONV_HALO, :] = jnp.where(i > 0, halo_ref[...], 0.0)
    ext_ref[CONV_HALO:, :] = cur_ref[...]
    first = CONV_HALO - (CONV_WIDTH - 1)
    D = cur_ref.shape[1]
    nc = D // LANES
    for b in range(SUBLANES):
        n = tm + (CONV_WIDTH - 1 - b) // SUBLANES * SUBLANES
        for c in range(nc):
            sh_ref[b, c, 0:n, :] = ext_ref[first + b:first + b + n, c * LANES:(c + 1) * LANES]

    def channel_block(c, carry):
        acc = jnp.broadcast_to(bdw_ref[c], (tm, LANES))
        for j in range(CONV_WIDTH):
            b = j % SUBLANES
            acc = acc + w_ref[j, c] * sh_ref[b, c, j - b:j - b + tm, :]
        y_ref[c] = acc
        return carry

    lax.fori_loop(0, nc, channel_block, 0)

    total = y_ref[0]
    for c in range(1, nc):
        total = total + y_ref[c]
    mu = jnp.sum(total, axis=-1, keepdims=True) * (1.0 / D)
    sq = jnp.square(y_ref[0] - mu)
    for c in range(1, nc):
        sq = sq + jnp.square(y_ref[c] - mu)
    rstd = lax.rsqrt(jnp.sum(sq, axis=-1, keepdims=True) * (1.0 / D) + LN_EPS)
    for c in range(nc):
        cs = slice(c * LANES, (c + 1) * LANES)
        y = (y_ref[c] - mu) * rstd * g_ref[:, cs] + b_ref[:, cs]
        o_ref[:, cs] = _silu(y).astype(BF16)


def _conv_ln_silu(u, w_dw, b_dw, g, b, tm=128):
    S, D = u.shape
    per = tm // CONV_HALO
    nc = D // LANES
    fixed = lambda i: (0, 0)
    return pl.pallas_call(
        functools.partial(_conv_kernel, tm=tm),
        grid=(S // tm,),
        in_specs=[pl.BlockSpec((tm, D), lambda i: (i, 0)),
                  pl.BlockSpec((CONV_HALO, D), lambda i: (jnp.maximum(i * per - 1, 0), 0)),
                  pl.BlockSpec((CONV_WIDTH, nc, 1, LANES), lambda i: (0, 0, 0, 0)),
                  pl.BlockSpec((nc, 1, LANES), lambda i: (0, 0, 0)),
                  pl.BlockSpec((1, D), fixed), pl.BlockSpec((1, D), fixed)],
        out_specs=pl.BlockSpec((tm, D), lambda i: (i, 0)),
        out_shape=jax.ShapeDtypeStruct((S, D), BF16),
        scratch_shapes=[pltpu.VMEM((tm + CONV_HALO, D), F32),
                        pltpu.VMEM((SUBLANES, nc, tm + CONV_HALO - SUBLANES, LANES), F32),
                        pltpu.VMEM((nc, tm, LANES), F32)],
        compiler_params=_params("parallel"),
    )(u, u, w_dw.reshape(CONV_WIDTH, nc, 1, LANES), b_dw.reshape(nc, 1, LANES), g.reshape(1, D), b.reshape(1, D))


def _xattn_kernel(q_ref, k_ref, v_ref, wo_ref, r_ref, g_ref, b_ref, of_ref, ob_ref):
    scale = XATTN_DH ** -0.5
    heads = []
    for h in range(XATTN_HEADS):
        cs = slice(h * XATTN_DH, (h + 1) * XATTN_DH)
        s = lax.dot_general(q_ref[:, cs], k_ref[:, cs], _NT, preferred_element_type=F32) * scale
        m = jnp.max(s, axis=-1, keepdims=True)
        p = jnp.exp(s - m)
        p = p * (1.0 / jnp.sum(p, axis=-1, keepdims=True))
        heads.append(_dot(p.astype(BF16), v_ref[:, cs]).astype(BF16))
    o = jnp.concatenate(heads, axis=-1)
    z = ALPHA * r_ref[...] + _dot(o, wo_ref[...])
    y = _layer_norm(z, g_ref[...], b_ref[...])
    of_ref[...] = y
    ob_ref[...] = y.astype(BF16)


def _xattn(q, k, v, wo, resid, g, b, tm=512):
    M, D = resid.shape
    row = lambda i: (i, 0)
    fixed = lambda i: (0, 0)
    return pl.pallas_call(
        _xattn_kernel,
        grid=(M // tm,),
        in_specs=[pl.BlockSpec((tm, D), row),
                  pl.BlockSpec(k.shape, fixed), pl.BlockSpec(v.shape, fixed),
                  pl.BlockSpec((D, D), fixed),
                  pl.BlockSpec((tm, D), row),
                  pl.BlockSpec((1, D), fixed), pl.BlockSpec((1, D), fixed)],
        out_specs=[pl.BlockSpec((tm, D), row), pl.BlockSpec((tm, D), row)],
        out_shape=[jax.ShapeDtypeStruct((M, D), F32), jax.ShapeDtypeStruct((M, D), BF16)],
        compiler_params=_params("parallel"),
    )(q, k, v, wo, resid, g.reshape(1, D), b.reshape(1, D))


def _lambda_init_for(layer):
    return 0.8 - 0.6 * math.exp(-0.3 * layer)


def kernel(x, mem, w_in, ret_gn_g, ret_gn_b, diff_lambda, diff_subln_g, w_mix_out, conv_w_pw1, conv_b_pw1, conv_w_dw, conv_b_dw, conv_ln_g, conv_ln_b, conv_w_pw2, conv_b_pw2, xattn_wq, xattn_wk, xattn_wv, xattn_wo, ffn_w_in, ffn_w_out, ln_g, ln_b):
    B, S, D = x.shape
    assert B == 1 and D == D_MODEL
    h = x[0]
    hb = h.astype(BF16)
    memb = mem[0].astype(BF16)
    zero_bias = jnp.zeros((D,), F32)
    for layer in range(DEPTH):
        last = layer == DEPTH - 1
        if layer % 2 == 0:
            e = layer // 2
            gate_blk, dq_blk = 3, 4
            col_scale = jnp.ones((1, w_in.shape[2]), F32).at[:, dq_blk * DIFF_W:(dq_blk + 1) * DIFF_W].set(
                DIFF_DK ** -0.5 * LOG2E)
            qkv = _matmul(hb, w_in, e, 6, BF16, src_block=lambda j: jnp.where(j >= gate_blk, j + 1, j),
                          col_scale=col_scale)
            gate = _matmul(hb, w_in, e, 1, F32, src_block=lambda j: j + gate_blk)
            ret = _retention(qkv, gate, ret_gn_g[e], ret_gn_b[e])
            dif = _diff_attention(qkv, diff_lambda[e], diff_subln_g[e], _lambda_init_for(layer + 1))
            wo = w_mix_out[e].astype(BF16)
            h, hb = _matmul_ln([ret, dif], [wo[:RET_W], wo[RET_W:]], zero_bias, h,
                               ln_g[layer, 0], ln_b[layer, 0])
        else:
            o = layer // 2
            u = _matmul_glu(hb, conv_w_pw1[o].astype(BF16), conv_b_pw1[o])
            c = _conv_ln_silu(u, conv_w_dw[o], conv_b_dw[o], conv_ln_g[o], conv_ln_b[o])
            h, hb = _matmul_ln([c], [conv_w_pw2[o].astype(BF16)], conv_b_pw2[o], h,
                               ln_g[layer, 0], ln_b[layer, 0])
        q = _matmul(hb, xattn_wq, layer, 2, BF16)
        k = _matmul(memb, xattn_wk, layer, 2, BF16)
        v = _matmul(memb, xattn_wv, layer, 2, BF16)
        h, hb = _xattn(q, k, v, xattn_wo[layer].astype(BF16), h, ln_g[layer, 1], ln_b[layer, 1])
        h, hb = _ffn(hb, ffn_w_in[layer].astype(BF16), ffn_w_out[layer].astype(BF16), h,
                     ln_g[layer, 2], ln_b[layer, 2], emit_bf16=not last)
    return h[None]
```

```python
import functools
import math

import jax
import jax.numpy as jnp
from jax import lax
from jax.experimental import pallas as pl
from jax.experimental.pallas import tpu as pltpu

F32 = jnp.float32
BF16 = jnp.bfloat16

D_MODEL = 2048
DEPTH = 2
RET_HEADS = 8
RET_DK = 128
RET_CHUNK = 128
RET_W = 1024
DIFF_HEADS = 4
DIFF_DK = 128
DIFF_DV = 256
DIFF_W = 1024
DIFF_ROW_CHUNK = 32
DIFF_KEY_TILES = 2
CONV_WIDTH = 31
CONV_HALO = 32
XATTN_HEADS = 4
XATTN_DH = D_MODEL // XATTN_HEADS
D_FF = 5632
ALPHA = (2.0 * DEPTH) ** 0.25
LN_EPS = 1e-5
NEG_BIG = -1e30
LOG2E = math.log2(math.e)
LANES = 128
SUBLANES = 8

VMEM_LIMIT_BYTES = 56 * 1024 * 1024

_NT = (((1,), (1,)), ((), ()))
_TN = (((0,), (0,)), ((), ()))


def _params(*sem):
    return pltpu.CompilerParams(dimension_semantics=sem, vmem_limit_bytes=VMEM_LIMIT_BYTES)


def _dot(a, b):
    return jnp.dot(a, b, preferred_element_type=F32)


def _layer_norm(z, g, b):
    mu = jnp.mean(z, axis=-1, keepdims=True)
    d = z - mu
    var = jnp.mean(d * d, axis=-1, keepdims=True)
    return d * lax.rsqrt(var + LN_EPS) * g + b


def _silu(x):
    return x * jax.nn.sigmoid(x)


def _mm_kernel(x_ref, w_ref, *rest, scaled):
    o_ref, wb_ref = rest[-2:]

    @pl.when(pl.program_id(1) == 0)
    def _():
        w = w_ref[...]
        if scaled:
            w = w * rest[0][...]
        wb_ref[...] = w.astype(BF16)

    o_ref[...] = _dot(x_ref[...], wb_ref[...]).astype(o_ref.dtype)


def _matmul(x, w3, lead, n_blocks, out_dtype, src_block=lambda j: j, col_scale=None, tm=1024, tn=1024):
    M, K = x.shape
    tm = min(tm, M)
    scaled = col_scale is not None
    in_specs = [pl.BlockSpec((tm, K), lambda j, i: (i, 0)),
                pl.BlockSpec((None, K, tn), lambda j, i: (lead, 0, src_block(j)))]
    args = [x, w3]
    if scaled:
        in_specs.append(pl.BlockSpec((1, tn), lambda j, i: (0, src_block(j))))
        args.append(col_scale)
    return pl.pallas_call(
        functools.partial(_mm_kernel, scaled=scaled),
        grid=(n_blocks, M // tm),
        in_specs=in_specs,
        out_specs=pl.BlockSpec((tm, tn), lambda j, i: (i, j)),
        out_shape=jax.ShapeDtypeStruct((M, n_blocks * tn), out_dtype),
        scratch_shapes=[pltpu.VMEM((K, tn), BF16)],
        compiler_params=_params("parallel", "arbitrary"),
    )(*args)


def _mm_glu_kernel(x_ref, wa_ref, wb_ref, ba_ref, bb_ref, o_ref):
    x = x_ref[...]
    a = _dot(x, wa_ref[...]) + ba_ref[...]
    b = _dot(x, wb_ref[...]) + bb_ref[...]
    o_ref[...] = a * jax.nn.sigmoid(b)


def _matmul_glu(x, w, bias, tm=2048, tn=512):
    M, K = x.shape
    tm = min(tm, M)
    N = w.shape[1] // 2
    nb = N // tn
    bias2 = bias.reshape(1, 2 * N)
    return pl.pallas_call(
        _mm_glu_kernel,
        grid=(M // tm, nb),
        in_specs=[pl.BlockSpec((tm, K), lambda i, j: (i, 0)),
                  pl.BlockSpec((K, tn), lambda i, j: (0, j)),
                  pl.BlockSpec((K, tn), lambda i, j: (0, j + nb)),
                  pl.BlockSpec((1, tn), lambda i, j: (0, j)),
                  pl.BlockSpec((1, tn), lambda i, j: (0, j + nb))],
        out_specs=pl.BlockSpec((tm, tn), lambda i, j: (i, j)),
        out_shape=jax.ShapeDtypeStruct((M, N), F32),
        compiler_params=_params("parallel", "arbitrary"),
    )(x, w, w, bias2, bias2)


def _mm_ln_kernel(*refs, n_pairs, emit_bf16):
    xs = refs[:n_pairs]
    ws = refs[n_pairs:2 * n_pairs]
    bias_ref, r_ref, g_ref, b_ref = refs[2 * n_pairs:2 * n_pairs + 4]
    outs = refs[2 * n_pairs + 4:]
    acc = _dot(xs[0][...], ws[0][...])
    for x_ref, w_ref in zip(xs[1:], ws[1:]):
        acc = acc + _dot(x_ref[...], w_ref[...])
    z = ALPHA * r_ref[...] + acc + bias_ref[...]
    y = _layer_norm(z, g_ref[...], b_ref[...])
    outs[0][...] = y
    if emit_bf16:
        outs[1][...] = y.astype(BF16)


def _matmul_ln(xs, ws, bias, resid, g, b, emit_bf16=True, tm=512):
    M, N = resid.shape
    n = len(xs)
    row = lambda i: (i, 0)
    fixed = lambda i: (0, 0)
    in_specs = ([pl.BlockSpec((tm, x.shape[1]), row) for x in xs]
                + [pl.BlockSpec(w.shape, fixed) for w in ws]
                + [pl.BlockSpec((1, N), fixed), pl.BlockSpec((tm, N), row),
                   pl.BlockSpec((1, N), fixed), pl.BlockSpec((1, N), fixed)])
    out_specs = [pl.BlockSpec((tm, N), row)]
    out_shape = [jax.ShapeDtypeStruct((M, N), F32)]
    if emit_bf16:
        out_specs.append(pl.BlockSpec((tm, N), row))
        out_shape.append(jax.ShapeDtypeStruct((M, N), BF16))
    outs = pl.pallas_call(
        functools.partial(_mm_ln_kernel, n_pairs=n, emit_bf16=emit_bf16),
        grid=(M // tm,),
        in_specs=in_specs, out_specs=out_specs, out_shape=out_shape,
        compiler_params=_params("parallel"),
    )(*xs, *ws, bias.reshape(1, N), resid, g.reshape(1, N), b.reshape(1, N))
    return outs if emit_bf16 else (outs[0], None)


def _ffn_kernel(x_ref, wg_ref, wu_ref, wo_ref, r_ref, g_ref, b_ref, *rest, nf, ne, emit_bf16):
    outs, a_ref, acc_ref = rest[:-2], rest[-2], rest[-1]
    f = pl.program_id(1)
    te = acc_ref.shape[0] // ne

    def up():
        x = x_ref[...]
        a_ref[...] = (_silu(_dot(x, wg_ref[...])) * _dot(x, wu_ref[...])).astype(BF16)

    def down():
        acc_ref[...] += _dot(a_ref[...], wo_ref[...])

    def epilogue(e):
        z = ALPHA * r_ref[...] + acc_ref[e * te:(e + 1) * te, :]
        y = _layer_norm(z, g_ref[...], b_ref[...])
        outs[0][...] = y
        if emit_bf16:
            outs[1][...] = y.astype(BF16)

    @pl.when(f == 0)
    def _():
        acc_ref[...] = jnp.zeros_like(acc_ref)
        up()

    @pl.when(jnp.logical_and(f > 0, f < nf))
    def _():
        down()
        up()

    @pl.when(f == nf)
    def _():
        down()
        epilogue(0)

    for e in range(1, ne):
        pl.when(f == nf + e)(functools.partial(epilogue, e))


def _ffn(xb, w_in, w_out, resid, g, b, emit_bf16=True, tm=1024, tf=512, ne=4):
    M, D = resid.shape
    tm = min(tm, M)
    nf = D_FF // tf
    te = tm // ne
    row = lambda i, f: (i, 0)
    fixed = lambda i, f: (0, 0)
    slab = lambda i, f: (i * ne + jnp.clip(f - nf, 0, ne - 1), 0)
    out_specs = [pl.BlockSpec((te, D), slab)]
    out_shape = [jax.ShapeDtypeStruct((M, D), F32)]
    if emit_bf16:
        out_specs.append(pl.BlockSpec((te, D), slab))
        out_shape.append(jax.ShapeDtypeStruct((M, D), BF16))
    outs = pl.pallas_call(
        functools.partial(_ffn_kernel, nf=nf, ne=ne, emit_bf16=emit_bf16),
        grid=(M // tm, nf + ne),
        in_specs=[pl.BlockSpec((tm, D), row),
                  pl.BlockSpec((D, tf), lambda i, f: (0, jnp.minimum(f, nf - 1))),
                  pl.BlockSpec((D, tf), lambda i, f: (0, jnp.minimum(f, nf - 1) + nf)),
                  pl.BlockSpec((tf, D), lambda i, f: (jnp.clip(f - 1, 0, nf - 1), 0)),
                  pl.BlockSpec((te, D), slab),
                  pl.BlockSpec((1, D), fixed), pl.BlockSpec((1, D), fixed)],
        out_specs=out_specs, out_shape=out_shape,
        scratch_shapes=[pltpu.VMEM((tm, tf), BF16), pltpu.VMEM((tm, D), F32)],
        compiler_params=_params("parallel", "arbitrary"),
    )(xb, w_in, w_in, w_out, resid, g.reshape(1, D), b.reshape(1, D))
    return outs if emit_bf16 else (outs[0], None)


def _ret_kernel(q_ref, k_ref, v_ref, gate_ref, gg_ref, gb_ref, o_ref, state_ref, *, nch):
    C = RET_CHUNK

    @pl.when(pl.program_id(0) == 0)
    def _():
        state_ref[...] = jnp.zeros_like(state_ref)

    scale = RET_DK ** -0.5
    row = lax.broadcasted_iota(jnp.int32, (C, C), 0).astype(F32)
    col = lax.broadcasted_iota(jnp.int32, (C, C), 1).astype(F32)
    dist = row - col
    for h in range(RET_HEADS):
        log_gamma = math.log1p(-(2.0 ** (-5.0 - h)))
        decay = jnp.where(dist >= 0, jnp.exp(log_gamma * jnp.maximum(dist, 0.0)), 0.0) * scale
        xi = jnp.exp(log_gamma * (row + 1.0))
        zeta = jnp.exp(log_gamma * (C - 1.0 - row)) * scale
        chunk_decay = math.exp(log_gamma * C)
        cs = slice(h * RET_DK, (h + 1) * RET_DK)
        gg = gg_ref[:, cs]
        gb = gb_ref[:, cs]
        for c in range(nch):
            rs = slice(c * C, (c + 1) * C)
            q = q_ref[rs, cs]
            k = k_ref[rs, cs]
            v = v_ref[rs, cs]
            state = state_ref[h]
            scores = lax.dot_general(q, k, _NT, preferred_element_type=F32) * decay
            y = _dot(scores.astype(BF16), v) + xi * _dot(q, state.astype(BF16))
            kz = (k.astype(F32) * zeta).astype(BF16)
            kv = lax.dot_general(kz, v, _TN, preferred_element_type=F32)
            state_ref[h] = state * chunk_decay + kv
            mu = jnp.mean(y, axis=-1, keepdims=True)
            d = y - mu
            var = jnp.mean(d * d, axis=-1, keepdims=True)
            yn = d * lax.rsqrt(var + LN_EPS) * gg + gb
            o_ref[rs, cs] = (_silu(gate_ref[rs, cs]) * yn).astype(BF16)


def _retention(qkv, gate, gn_g, gn_b, tr=512):
    S = qkv.shape[0]
    tr = min(tr, S)
    fixed = lambda i: (0, 0)
    return pl.pallas_call(
        functools.partial(_ret_kernel, nch=tr // RET_CHUNK),
        grid=(S // tr,),
        in_specs=[pl.BlockSpec((tr, RET_W), lambda i: (i, 0)),
                  pl.BlockSpec((tr, RET_W), lambda i: (i, 1)),
                  pl.BlockSpec((tr, RET_W), lambda i: (i, 2)),
                  pl.BlockSpec((tr, RET_W), lambda i: (i, 0)),
                  pl.BlockSpec((1, RET_W), fixed), pl.BlockSpec((1, RET_W), fixed)],
        out_specs=pl.BlockSpec((tr, RET_W), lambda i: (i, 0)),
        out_shape=jax.ShapeDtypeStruct((S, RET_W), BF16),
        scratch_shapes=[pltpu.VMEM((RET_HEADS, RET_DK, RET_DK), F32)],
        compiler_params=_params("arbitrary"),
    )(qkv, qkv, qkv, gate, gn_g.reshape(1, RET_W), gn_b.reshape(1, RET_W))


def _diff_kernel(lam_ref, q_ref, k_ref, v_ref, g_ref, o_ref, kb_ref, s_ref, p_ref, alpha_ref,
                 m_ref, l_ref, acc_ref, *, tq, lambda_init):
    h = pl.program_id(0)
    i = pl.program_id(1)
    tk = DIFF_KEY_TILES * tq
    n = i // DIFF_KEY_TILES
    m_ref[...] = jnp.full(m_ref.shape, NEG_BIG, F32)
    l_ref[...] = jnp.zeros_like(l_ref)
    acc_ref[...] = jnp.zeros_like(acc_ref)

    head = (h + 1).astype(F32)
    slope = jnp.exp2(jnp.full((tk, LANES), -8.0 / DIFF_HEADS, F32) * head) * LOG2E
    lane = lax.broadcasted_iota(jnp.int32, (tk, LANES), 1)

    def split3(x, first_lane, lanes):
        hi = x.astype(BF16).astype(F32)
        mid = (x - hi).astype(BF16).astype(F32)
        lo = x - hi - mid
        pieces = jnp.where(lanes == first_lane, hi,
                           jnp.where(lanes == first_lane + 1, mid,
                                     jnp.where(lanes == first_lane + 2, lo, 0.0)))
        return pieces.astype(BF16)

    within = slope * lax.broadcasted_iota(jnp.int32, (tk, LANES), 0).astype(F32)
    kb_ref[...] = jnp.where(jnp.logical_and(lane >= 3, lane < 6), 1.0, split3(within, 0, lane).astype(F32)).astype(BF16)
    q_ones = jnp.where(lane[:SUBLANES] < 3, 1.0, 0.0)

    def scores(j, par):
        ks = k_ref[pl.ds(pl.multiple_of(j * tk, tk), tk), :]
        tile_bias = split3(slope[:SUBLANES] * (j * tk - i * tq).astype(F32), 3, lane[:SUBLANES]).astype(F32)
        qb = jnp.tile(q_ones + tile_bias, (tq // SUBLANES, 1)).astype(BF16)
        kb = kb_ref[...]
        for mi in range(2):
            ms = slice(mi * DIFF_DK, (mi + 1) * DIFF_DK)
            qa = jnp.concatenate([q_ref[:, ms], qb], axis=1)
            ka = jnp.concatenate([ks[:, ms], kb], axis=1)
            s_ref[par, mi] = lax.dot_general(qa, ka, _NT, preferred_element_type=F32)

    def softmax(j, par, masked):
        for mi in range(2):
            for r0 in range(0, tq, DIFF_ROW_CHUNK):
                rs = slice(r0, r0 + DIFF_ROW_CHUNK)
                s = s_ref[par, mi, rs, :]
                if masked:
                    r = lax.broadcasted_iota(jnp.int32, (DIFF_ROW_CHUNK, tk), 0) + (r0 + i * tq)
                    c = lax.broadcasted_iota(jnp.int32, (DIFF_ROW_CHUNK, tk), 1) + j * tk
                    s = jnp.where(r >= c, s, NEG_BIG)
                m_old = m_ref[mi, rs, :]
                m_new = jnp.maximum(m_old, jnp.max(s, axis=-1, keepdims=True))
                p = jnp.exp2(s - jnp.tile(m_new, (1, tk // LANES)))
                alpha = jnp.exp2(m_old - m_new)
                psum = p[:, :LANES]
                for t in range(1, tk // LANES):
                    psum = psum + p[:, t * LANES:(t + 1) * LANES]
                l_ref[mi, rs, :] = alpha * l_ref[mi, rs, :] + psum
                m_ref[mi, rs, :] = m_new
                alpha_ref[par, mi, rs, :] = alpha
                p_ref[par, mi, rs, :] = p.astype(BF16)

    def accumulate(j, par):
        vs = v_ref[pl.ds(pl.multiple_of(j * tk, tk), tk), :]
        for mi in range(2):
            acc_ref[mi] = (jnp.tile(alpha_ref[par, mi], (1, DIFF_DV // LANES)) * acc_ref[mi]
                           + _dot(p_ref[par, mi], vs))

    scores(0, 0)

    @pl.when(n == 0)
    def _():
        softmax(0, 0, masked=True)
        accumulate(0, 0)

    @pl.when(n > 0)
    def _():
        scores(1, 1)
        softmax(0, 0, masked=False)

        def pair(t, carry):
            j = 2 * t + 1
            scores(j + 1, 0)
            softmax(j, 1, masked=False)
            accumulate(j - 1, 0)
            scores(j + 2, 1)
            softmax(j + 1, 0, masked=False)
            accumulate(j, 1)
            return carry

        lax.fori_loop(0, (n - 1) // 2, pair, 0)

        @pl.when(n % 2 == 0)
        def _():
            scores(n, 0)
            softmax(n - 1, 1, masked=False)
            accumulate(n - 2, 0)
            softmax(n, 0, masked=True)
            accumulate(n - 1, 1)
            accumulate(n, 0)

        @pl.when(n % 2 == 1)
        def _():
            softmax(n, 1, masked=True)
            accumulate(n - 1, 0)
            accumulate(n, 1)

    lf = lam_ref[...]
    lam = (jnp.exp(jnp.sum(lf[0:1] * lf[1:2], axis=-1, keepdims=True))
           - jnp.exp(jnp.sum(lf[2:3] * lf[3:4], axis=-1, keepdims=True)) + lambda_init)
    inv1 = 1.0 / jnp.sum(l_ref[0], axis=-1, keepdims=True)
    inv2 = 1.0 / jnp.sum(l_ref[1], axis=-1, keepdims=True)
    o = acc_ref[0] * inv1 - lam * (acc_ref[1] * inv2)
    y = o * lax.rsqrt(jnp.mean(o * o, axis=-1, keepdims=True) + LN_EPS)
    o_ref[...] = (y * g_ref[...] * (1.0 - lambda_init)).astype(BF16)


def _diff_attention(qkv, diff_lambda, subln_g, lambda_init, tq=512):
    S = qkv.shape[0]
    tq = min(tq, S // DIFF_KEY_TILES)
    tk = DIFF_KEY_TILES * tq
    qoff = 3 * RET_W // DIFF_DV
    koff = qoff + DIFF_W // DIFF_DV
    voff = koff + DIFF_W // DIFF_DV
    return pl.pallas_call(
        functools.partial(_diff_kernel, tq=tq, lambda_init=lambda_init),
        grid=(DIFF_HEADS, S // tq),
        in_specs=[pl.BlockSpec((4, DIFF_DK), lambda h, i: (0, 0)),
                  pl.BlockSpec((tq, DIFF_DV), lambda h, i: (i, qoff + h)),
                  pl.BlockSpec((S, DIFF_DV), lambda h, i: (0, koff + h), pipeline_mode=pl.Buffered(1)),
                  pl.BlockSpec((S, DIFF_DV), lambda h, i: (0, voff + h), pipeline_mode=pl.Buffered(1)),
                  pl.BlockSpec((1, DIFF_DV), lambda h, i: (0, h))],
        out_specs=pl.BlockSpec((tq, DIFF_DV), lambda h, i: (i, h)),
        out_shape=jax.ShapeDtypeStruct((S, DIFF_W), BF16),
        scratch_shapes=[pltpu.VMEM((tk, LANES), BF16),
                        pltpu.VMEM((2, 2, tq, tk), F32),
                        pltpu.VMEM((2, 2, tq, tk), BF16), pltpu.VMEM((2, 2, tq, LANES), F32),
                        pltpu.VMEM((2, tq, LANES), F32), pltpu.VMEM((2, tq, LANES), F32),
                        pltpu.VMEM((2, tq, DIFF_DV), F32)],
        compiler_params=_params("parallel", "arbitrary"),
    )(diff_lambda, qkv, qkv, qkv, subln_g.reshape(1, DIFF_W))


def _conv_kernel(cur_ref, halo_ref, w_ref, bdw_ref, g_ref, b_ref, o_ref, ext_ref, sh_ref, y_ref, *, tm):
    i = pl.program_id(0)
    ext_ref[0:CONV_HALO, :] = jnp.where(i > 0, halo_ref[...], 0.0)
    ext_ref[CONV_HALO:, :] = cur_ref[...]
    first = CONV_HALO - (CONV_WIDTH - 1)
    D = cur_ref.shape[1]
    nc = D // LANES
    for b in range(SUBLANES):
        n = tm + (CONV_WIDTH - 1 - b) // SUBLANES * SUBLANES
        for c in range(nc):
            sh_ref[b, c, 0:n, :] = ext_ref[first + b:first + b + n, c * LANES:(c + 1) * LANES]

    def channel_block(c, carry):
        acc = jnp.broadcast_to(bdw_ref[c], (tm, LANES))
        for j in range(CONV_WIDTH):
            b = j % SUBLANES
            acc = acc + w_ref[j, c] * sh_ref[b, c, j - b:j - b + tm, :]
        y_ref[c] = acc
        return carry

    lax.fori_loop(0, nc, channel_block, 0)

    total = y_ref[0]
    for c in range(1, nc):
        total = total + y_ref[c]
    mu = jnp.sum(total, axis=-1, keepdims=True) * (1.0 / D)
    sq = jnp.square(y_ref[0] - mu)
    for c in range(1, nc):
        sq = sq + jnp.square(y_ref[c] - mu)
    rstd = lax.rsqrt(jnp.sum(sq, axis=-1, keepdims=True) * (1.0 / D) + LN_EPS)
    for c in range(nc):
        cs = slice(c * LANES, (c + 1) * LANES)
        y = (y_ref[c] - mu) * rstd * g_ref[:, cs] + b_ref[:, cs]
        o_ref[:, cs] = _silu(y).astype(BF16)


def _conv_ln_silu(u, w_dw, b_dw, g, b, tm=128):
    S, D = u.shape
    per = tm // CONV_HALO
    nc = D // LANES
    fixed = lambda i: (0, 0)
    return pl.pallas_call(
        functools.partial(_conv_kernel, tm=tm),
        grid=(S // tm,),
        in_specs=[pl.BlockSpec((tm, D), lambda i: (i, 0)),
                  pl.BlockSpec((CONV_HALO, D), lambda i: (jnp.maximum(i * per - 1, 0), 0)),
                  pl.BlockSpec((CONV_WIDTH, nc, 1, LANES), lambda i: (0, 0, 0, 0)),
                  pl.BlockSpec((nc, 1, LANES), lambda i: (0, 0, 0)),
                  pl.BlockSpec((1, D), fixed), pl.BlockSpec((1, D), fixed)],
        out_specs=pl.BlockSpec((tm, D), lambda i: (i, 0)),
        out_shape=jax.ShapeDtypeStruct((S, D), BF16),
        scratch_shapes=[pltpu.VMEM((tm + CONV_HALO, D), F32),
                        pltpu.VMEM((SUBLANES, nc, tm + CONV_HALO - SUBLANES, LANES), F32),
                        pltpu.VMEM((nc, tm, LANES), F32)],
        compiler_params=_params("parallel"),
    )(u, u, w_dw.reshape(CONV_WIDTH, nc, 1, LANES), b_dw.reshape(nc, 1, LANES), g.reshape(1, D), b.reshape(1, D))


def _xattn_kernel(x_ref, wq_ref, k_ref, v_ref, wo_ref, r_ref, g_ref, b_ref, of_ref, ob_ref):
    q = _dot(x_ref[...], wq_ref[...]).astype(BF16)
    heads = []
    for h in range(XATTN_HEADS):
        cs = slice(h * XATTN_DH, (h + 1) * XATTN_DH)
        s = lax.dot_general(q[:, cs], k_ref[:, cs], _NT, preferred_element_type=F32)
        m = jnp.max(s, axis=-1, keepdims=True)
        p = jnp.exp2(s - m)
        p = p * (1.0 / jnp.sum(p, axis=-1, keepdims=True))
        heads.append(_dot(p.astype(BF16), v_ref[:, cs]).astype(BF16))
    o = jnp.concatenate(heads, axis=-1)
    z = ALPHA * r_ref[...] + _dot(o, wo_ref[...])
    y = _layer_norm(z, g_ref[...], b_ref[...])
    of_ref[...] = y
    ob_ref[...] = y.astype(BF16)


def _xattn(xb, wq, k, v, wo, resid, g, b, tm=512):
    M, D = resid.shape
    row = lambda i: (i, 0)
    fixed = lambda i: (0, 0)
    resident = lambda shape: pl.BlockSpec(shape, fixed, pipeline_mode=pl.Buffered(1))
    return pl.pallas_call(
        _xattn_kernel,
        grid=(M // tm,),
        in_specs=[pl.BlockSpec((tm, D), row),
                  resident((D, D)), resident(k.shape), resident(v.shape), resident((D, D)),
                  pl.BlockSpec((tm, D), row),
                  pl.BlockSpec((1, D), fixed), pl.BlockSpec((1, D), fixed)],
        out_specs=[pl.BlockSpec((tm, D), row), pl.BlockSpec((tm, D), row)],
        out_shape=[jax.ShapeDtypeStruct((M, D), F32), jax.ShapeDtypeStruct((M, D), BF16)],
        compiler_params=_params("parallel"),
    )(xb, wq, k, v, wo, resid, g.reshape(1, D), b.reshape(1, D))


def _lambda_init_for(layer):
    return 0.8 - 0.6 * math.exp(-0.3 * layer)


def kernel(x, mem, w_in, ret_gn_g, ret_gn_b, diff_lambda, diff_subln_g, w_mix_out, conv_w_pw1, conv_b_pw1, conv_w_dw, conv_b_dw, conv_ln_g, conv_ln_b, conv_w_pw2, conv_b_pw2, xattn_wq, xattn_wk, xattn_wv, xattn_wo, ffn_w_in, ffn_w_out, ln_g, ln_b):
    B, S, D = x.shape
    assert B == 1 and D == D_MODEL
    h = x[0]
    hb = h.astype(BF16)
    memb = mem[0].astype(BF16)
    zero_bias = jnp.zeros((D,), F32)
    for layer in range(DEPTH):
        last = layer == DEPTH - 1
        if layer % 2 == 0:
            e = layer // 2
            gate_blk, dq_blk = 3, 4
            col_scale = jnp.ones((1, w_in.shape[2]), F32).at[:, dq_blk * DIFF_W:(dq_blk + 1) * DIFF_W].set(
                DIFF_DK ** -0.5 * LOG2E)
            qkv = _matmul(hb, w_in, e, 6, BF16, src_block=lambda j: jnp.where(j >= gate_blk, j + 1, j),
                          col_scale=col_scale)
            gate = _matmul(hb, w_in, e, 1, F32, src_block=lambda j: j + gate_blk)
            ret = _retention(qkv, gate, ret_gn_g[e], ret_gn_b[e])
            dif = _diff_attention(qkv, diff_lambda[e], diff_subln_g[e], _lambda_init_for(layer + 1))
            wo = w_mix_out[e].astype(BF16)
            h, hb = _matmul_ln([ret, dif], [wo[:RET_W], wo[RET_W:]], zero_bias, h,
                               ln_g[layer, 0], ln_b[layer, 0])
        else:
            o = layer // 2
            u = _matmul_glu(hb, conv_w_pw1[o].astype(BF16), conv_b_pw1[o])
            c = _conv_ln_silu(u, conv_w_dw[o], conv_b_dw[o], conv_ln_g[o], conv_ln_b[o])
            h, hb = _matmul_ln([c], [conv_w_pw2[o].astype(BF16)], conv_b_pw2[o], h,
                               ln_g[layer, 0], ln_b[layer, 0])
        k = _matmul(memb, xattn_wk, layer, 2, BF16)
        v = _matmul(memb, xattn_wv, layer, 2, BF16)
        wq = (xattn_wq[layer] * (XATTN_DH ** -0.5 * LOG2E)).astype(BF16)
        h, hb = _xattn(hb, wq, k, v, xattn_wo[layer].astype(BF16), h, ln_g[layer, 1], ln_b[layer, 1])
        h, hb = _ffn(hb, ffn_w_in[layer].astype(BF16), ffn_w_out[layer].astype(BF16), h,
                     ln_g[layer, 2], ln_b[layer, 2], emit_bf16=not last)
    return h[None]
```

```python
import functools
import math

import jax
import jax.numpy as jnp
from jax import lax
from jax.experimental import pallas as pl
from jax.experimental.pallas import tpu as pltpu

F32 = jnp.float32
BF16 = jnp.bfloat16

D_MODEL = 2048
DEPTH = 2
RET_HEADS = 8
RET_DK = 128
RET_CHUNK = 128
RET_W = 1024
DIFF_HEADS = 4
DIFF_DK = 128
DIFF_DV = 256
DIFF_W = 1024
DIFF_ROW_CHUNK = 32
DIFF_KEY_TILES = 2
CONV_WIDTH = 31
CONV_HALO = 32
XATTN_HEADS = 4
XATTN_DH = D_MODEL // XATTN_HEADS
D_FF = 5632
ALPHA = (2.0 * DEPTH) ** 0.25
LN_EPS = 1e-5
NEG_BIG = -1e30
LOG2E = math.log2(math.e)
LANES = 128
SUBLANES = 8

VMEM_LIMIT_BYTES = 56 * 1024 * 1024

_NT = (((1,), (1,)), ((), ()))
_TN = (((0,), (0,)), ((), ()))


def _params(*sem):
    return pltpu.CompilerParams(dimension_semantics=sem, vmem_limit_bytes=VMEM_LIMIT_BYTES)


def _dot(a, b):
    return jnp.dot(a, b, preferred_element_type=F32)


def _layer_norm(z, g, b):
    mu = jnp.mean(z, axis=-1, keepdims=True)
    d = z - mu
    var = jnp.mean(d * d, axis=-1, keepdims=True)
    return d * lax.rsqrt(var + LN_EPS) * g + b


def _silu(x):
    return x * jax.nn.sigmoid(x)


def _mm_kernel(x_ref, w_ref, *rest, scaled):
    o_ref, wb_ref = rest[-2:]

    @pl.when(pl.program_id(1) == 0)
    def _():
        w = w_ref[...]
        if scaled:
            w = w * rest[0][...]
        wb_ref[...] = w.astype(BF16)

    o_ref[...] = _dot(x_ref[...], wb_ref[...]).astype(o_ref.dtype)


def _matmul(x, w3, lead, n_blocks, out_dtype, src_block=lambda j: j, col_scale=None, tm=1024, tn=1024):
    M, K = x.shape
    tm = min(tm, M)
    scaled = col_scale is not None
    in_specs = [pl.BlockSpec((tm, K), lambda j, i: (i, 0)),
                pl.BlockSpec((None, K, tn), lambda j, i: (lead, 0, src_block(j)))]
    args = [x, w3]
    if scaled:
        in_specs.append(pl.BlockSpec((1, tn), lambda j, i: (0, src_block(j))))
        args.append(col_scale)
    return pl.pallas_call(
        functools.partial(_mm_kernel, scaled=scaled),
        grid=(n_blocks, M // tm),
        in_specs=in_specs,
        out_specs=pl.BlockSpec((tm, tn), lambda j, i: (i, j)),
        out_shape=jax.ShapeDtypeStruct((M, n_blocks * tn), out_dtype),
        scratch_shapes=[pltpu.VMEM((K, tn), BF16)],
        compiler_params=_params("parallel", "arbitrary"),
    )(*args)


def _mm_glu_kernel(x_ref, wa_ref, wb_ref, ba_ref, bb_ref, o_ref):
    x = x_ref[...]
    a = _dot(x, wa_ref[...]) + ba_ref[...]
    b = _dot(x, wb_ref[...]) + bb_ref[...]
    o_ref[...] = a * jax.nn.sigmoid(b)


def _matmul_glu(x, w, bias, tm=2048, tn=512):
    M, K = x.shape
    tm = min(tm, M)
    N = w.shape[1] // 2
    nb = N // tn
    bias2 = bias.reshape(1, 2 * N)
    return pl.pallas_call(
        _mm_glu_kernel,
        grid=(M // tm, nb),
        in_specs=[pl.BlockSpec((tm, K), lambda i, j: (i, 0)),
                  pl.BlockSpec((K, tn), lambda i, j: (0, j)),
                  pl.BlockSpec((K, tn), lambda i, j: (0, j + nb)),
                  pl.BlockSpec((1, tn), lambda i, j: (0, j)),
                  pl.BlockSpec((1, tn), lambda i, j: (0, j + nb))],
        out_specs=pl.BlockSpec((tm, tn), lambda i, j: (i, j)),
        out_shape=jax.ShapeDtypeStruct((M, N), F32),
        compiler_params=_params("parallel", "arbitrary"),
    )(x, w, w, bias2, bias2)


def _mm_ln_kernel(*refs, n_pairs, emit_bf16):
    xs = refs[:n_pairs]
    ws = refs[n_pairs:2 * n_pairs]
    bias_ref, r_ref, g_ref, b_ref = refs[2 * n_pairs:2 * n_pairs + 4]
    outs = refs[2 * n_pairs + 4:]
    acc = _dot(xs[0][...], ws[0][...])
    for x_ref, w_ref in zip(xs[1:], ws[1:]):
        acc = acc + _dot(x_ref[...], w_ref[...])
    z = ALPHA * r_ref[...] + acc + bias_ref[...]
    y = _layer_norm(z, g_ref[...], b_ref[...])
    outs[0][...] = y
    if emit_bf16:
        outs[1][...] = y.astype(BF16)


def _matmul_ln(xs, ws, bias, resid, g, b, emit_bf16=True, tm=512):
    M, N = resid.shape
    n = len(xs)
    row = lambda i: (i, 0)
    fixed = lambda i: (0, 0)
    in_specs = ([pl.BlockSpec((tm, x.shape[1]), row) for x in xs]
                + [pl.BlockSpec(w.shape, fixed) for w in ws]
                + [pl.BlockSpec((1, N), fixed), pl.BlockSpec((tm, N), row),
                   pl.BlockSpec((1, N), fixed), pl.BlockSpec((1, N), fixed)])
    out_specs = [pl.BlockSpec((tm, N), row)]
    out_shape = [jax.ShapeDtypeStruct((M, N), F32)]
    if emit_bf16:
        out_specs.append(pl.BlockSpec((tm, N), row))
        out_shape.append(jax.ShapeDtypeStruct((M, N), BF16))
    outs = pl.pallas_call(
        functools.partial(_mm_ln_kernel, n_pairs=n, emit_bf16=emit_bf16),
        grid=(M // tm,),
        in_specs=in_specs, out_specs=out_specs, out_shape=out_shape,
        compiler_params=_params("parallel"),
    )(*xs, *ws, bias.reshape(1, N), resid, g.reshape(1, N), b.reshape(1, N))
    return outs if emit_bf16 else (outs[0], None)


def _ffn_kernel(x_ref, wg_ref, wu_ref, wo_ref, r_ref, g_ref, b_ref, *rest, nrb, nf, ne, emit_bf16):
    outs, a_ref, acc_ref = rest[:-2], rest[-2], rest[-1]
    i = pl.program_id(0)
    f = pl.program_id(1)
    cur = i % 2
    te = acc_ref.shape[1] // ne
    active = i < nrb

    def up():
        x = x_ref[...]
        a_ref[...] = (_silu(_dot(x, wg_ref[...])) * _dot(x, wu_ref[...])).astype(BF16)

    def down():
        acc_ref[cur] += _dot(a_ref[...], wo_ref[...])

    def epilogue(e):
        z = ALPHA * r_ref[...] + acc_ref[1 - cur, e * te:(e + 1) * te, :]
        y = _layer_norm(z, g_ref[...], b_ref[...])
        outs[0][...] = y
        if emit_bf16:
            outs[1][...] = y.astype(BF16)

    for e in range(ne):
        pl.when(jnp.logical_and(i > 0, f == e + 1))(functools.partial(epilogue, e))

    @pl.when(jnp.logical_and(active, f == 0))
    def _():
        acc_ref[cur] = jnp.zeros(acc_ref.shape[1:], F32)
        up()

    @pl.when(jnp.logical_and(active, jnp.logical_and(f > 0, f < nf)))
    def _():
        down()
        up()

    @pl.when(jnp.logical_and(active, f == nf))
    def _():
        down()


def _ffn(xb, w_in, w_out, resid, g, b, emit_bf16=True, tm=1024, tf=512, ne=4):
    M, D = resid.shape
    tm = min(tm, M)
    nrb = M // tm
    nf = D_FF // tf
    te = tm // ne
    assert ne <= nf
    fixed = lambda i, f: (0, 0)
    chunk = lambda i, f: jnp.where(i < nrb, jnp.minimum(f, nf - 1), nf - 1)
    slab = lambda i, f: (jnp.where(i == 0, 0, (i - 1) * ne + jnp.clip(f - 1, 0, ne - 1)), 0)
    out_specs = [pl.BlockSpec((te, D), slab)]
    out_shape = [jax.ShapeDtypeStruct((M, D), F32)]
    if emit_bf16:
        out_specs.append(pl.BlockSpec((te, D), slab))
        out_shape.append(jax.ShapeDtypeStruct((M, D), BF16))
    outs = pl.pallas_call(
        functools.partial(_ffn_kernel, nrb=nrb, nf=nf, ne=ne, emit_bf16=emit_bf16),
        grid=(nrb + 1, nf + 1),
        in_specs=[pl.BlockSpec((tm, D), lambda i, f: (jnp.minimum(i, nrb - 1), 0)),
                  pl.BlockSpec((D, tf), lambda i, f: (0, chunk(i, f))),
                  pl.BlockSpec((D, tf), lambda i, f: (0, chunk(i, f) + nf)),
                  pl.BlockSpec((tf, D), lambda i, f: (jnp.where(i < nrb, jnp.clip(f - 1, 0, nf - 1), nf - 1), 0)),
                  pl.BlockSpec((te, D), slab),
                  pl.BlockSpec((1, D), fixed), pl.BlockSpec((1, D), fixed)],
        out_specs=out_specs, out_shape=out_shape,
        scratch_shapes=[pltpu.VMEM((tm, tf), BF16), pltpu.VMEM((2, tm, D), F32)],
        compiler_params=_params("arbitrary", "arbitrary"),
    )(xb, w_in, w_in, w_out, resid, g.reshape(1, D), b.reshape(1, D))
    return outs if emit_bf16 else (outs[0], None)


def _ret_kernel(q_ref, k_ref, v_ref, gate_ref, gg_ref, gb_ref, o_ref, state_ref, *, nch):
    C = RET_CHUNK

    @pl.when(pl.program_id(0) == 0)
    def _():
        state_ref[...] = jnp.zeros_like(state_ref)

    scale = RET_DK ** -0.5
    row = lax.broadcasted_iota(jnp.int32, (C, C), 0).astype(F32)
    col = lax.broadcasted_iota(jnp.int32, (C, C), 1).astype(F32)
    dist = row - col
    for h in range(RET_HEADS):
        log_gamma = math.log1p(-(2.0 ** (-5.0 - h)))
        decay = jnp.where(dist >= 0, jnp.exp(log_gamma * jnp.maximum(dist, 0.0)), 0.0) * scale
        xi = jnp.exp(log_gamma * (row + 1.0))
        zeta = jnp.exp(log_gamma * (C - 1.0 - row)) * scale
        chunk_decay = math.exp(log_gamma * C)
        cs = slice(h * RET_DK, (h + 1) * RET_DK)
        gg = gg_ref[:, cs]
        gb = gb_ref[:, cs]
        for c in range(nch):
            rs = slice(c * C, (c + 1) * C)
            q = q_ref[rs, cs]
            k = k_ref[rs, cs]
            v = v_ref[rs, cs]
            state = state_ref[h]
            scores = lax.dot_general(q, k, _NT, preferred_element_type=F32) * decay
            y = _dot(scores.astype(BF16), v) + xi * _dot(q, state.astype(BF16))
            kz = (k.astype(F32) * zeta).astype(BF16)
            kv = lax.dot_general(kz, v, _TN, preferred_element_type=F32)
            state_ref[h] = state * chunk_decay + kv
            mu = jnp.mean(y, axis=-1, keepdims=True)
            d = y - mu
            var = jnp.mean(d * d, axis=-1, keepdims=True)
            yn = d * lax.rsqrt(var + LN_EPS) * gg + gb
            o_ref[rs, cs] = (_silu(gate_ref[rs, cs]) * yn).astype(BF16)


def _retention(qkv, gate, gn_g, gn_b, tr=512):
    S = qkv.shape[0]
    tr = min(tr, S)
    fixed = lambda i: (0, 0)
    return pl.pallas_call(
        functools.partial(_ret_kernel, nch=tr // RET_CHUNK),
        grid=(S // tr,),
        in_specs=[pl.BlockSpec((tr, RET_W), lambda i: (i, 0)),
                  pl.BlockSpec((tr, RET_W), lambda i: (i, 1)),
                  pl.BlockSpec((tr, RET_W), lambda i: (i, 2)),
                  pl.BlockSpec((tr, RET_W), lambda i: (i, 0)),
                  pl.BlockSpec((1, RET_W), fixed), pl.BlockSpec((1, RET_W), fixed)],
        out_specs=pl.BlockSpec((tr, RET_W), lambda i: (i, 0)),
        out_shape=jax.ShapeDtypeStruct((S, RET_W), BF16),
        scratch_shapes=[pltpu.VMEM((RET_HEADS, RET_DK, RET_DK), F32)],
        compiler_params=_params("arbitrary"),
    )(qkv, qkv, qkv, gate, gn_g.reshape(1, RET_W), gn_b.reshape(1, RET_W))


def _diff_kernel(lam_ref, q_ref, k_ref, v_ref, g_ref, o_ref, kb_ref, s_ref, p_ref, alpha_ref,
                 m_ref, l_ref, acc_ref, *, tq, lambda_init):
    h = pl.program_id(0)
    i = pl.program_id(1)
    tk = DIFF_KEY_TILES * tq
    n = i // DIFF_KEY_TILES
    m_ref[...] = jnp.full(m_ref.shape, NEG_BIG, F32)
    l_ref[...] = jnp.zeros_like(l_ref)
    acc_ref[...] = jnp.zeros_like(acc_ref)

    head = (h + 1).astype(F32)
    slope = jnp.exp2(jnp.full((tk, LANES), -8.0 / DIFF_HEADS, F32) * head) * LOG2E
    lane = lax.broadcasted_iota(jnp.int32, (tk, LANES), 1)

    def split3(x, first_lane, lanes):
        hi = x.astype(BF16).astype(F32)
        mid = (x - hi).astype(BF16).astype(F32)
        lo = x - hi - mid
        pieces = jnp.where(lanes == first_lane, hi,
                           jnp.where(lanes == first_lane + 1, mid,
                                     jnp.where(lanes == first_lane + 2, lo, 0.0)))
        return pieces.astype(BF16)

    within = slope * lax.broadcasted_iota(jnp.int32, (tk, LANES), 0).astype(F32)
    kb_ref[...] = jnp.where(jnp.logical_and(lane >= 3, lane < 6), 1.0, split3(within, 0, lane).astype(F32)).astype(BF16)
    q_ones = jnp.where(lane[:SUBLANES] < 3, 1.0, 0.0)

    def scores(j, par):
        ks = k_ref[pl.ds(pl.multiple_of(j * tk, tk), tk), :]
        tile_bias = split3(slope[:SUBLANES] * (j * tk - i * tq).astype(F32), 3, lane[:SUBLANES]).astype(F32)
        qb = jnp.tile(q_ones + tile_bias, (tq // SUBLANES, 1)).astype(BF16)
        kb = kb_ref[...]
        for mi in range(2):
            ms = slice(mi * DIFF_DK, (mi + 1) * DIFF_DK)
            qa = jnp.concatenate([q_ref[:, ms], qb], axis=1)
            ka = jnp.concatenate([ks[:, ms], kb], axis=1)
            s_ref[par, mi] = lax.dot_general(qa, ka, _NT, preferred_element_type=F32)

    def softmax(j, par, masked):
        for mi in range(2):
            for r0 in range(0, tq, DIFF_ROW_CHUNK):
                rs = slice(r0, r0 + DIFF_ROW_CHUNK)
                s = s_ref[par, mi, rs, :]
                if masked:
                    r = lax.broadcasted_iota(jnp.int32, (DIFF_ROW_CHUNK, tk), 0) + (r0 + i * tq)
                    c = lax.broadcasted_iota(jnp.int32, (DIFF_ROW_CHUNK, tk), 1) + j * tk
                    s = jnp.where(r >= c, s, NEG_BIG)
                m_old = m_ref[mi, rs, :]
                m_new = jnp.maximum(m_old, jnp.max(s, axis=-1, keepdims=True))
                p = jnp.exp2(s - jnp.tile(m_new, (1, tk // LANES)))
                alpha = jnp.exp2(m_old - m_new)
                psum = p[:, :LANES]
                for t in range(1, tk // LANES):
                    psum = psum + p[:, t * LANES:(t + 1) * LANES]
                l_ref[mi, rs, :] = alpha * l_ref[mi, rs, :] + psum
                m_ref[mi, rs, :] = m_new
                alpha_ref[par, mi, rs, :] = alpha
                p_ref[par, mi, rs, :] = p.astype(BF16)

    def accumulate(j, par):
        vs = v_ref[pl.ds(pl.multiple_of(j * tk, tk), tk), :]
        for mi in range(2):
            acc_ref[mi] = (jnp.tile(alpha_ref[par, mi], (1, DIFF_DV // LANES)) * acc_ref[mi]
                           + _dot(p_ref[par, mi], vs))

    scores(0, 0)

    @pl.when(n == 0)
    def _():
        softmax(0, 0, masked=True)
        accumulate(0, 0)

    @pl.when(n > 0)
    def _():
        scores(1, 1)
        softmax(0, 0, masked=False)

        def pair(t, carry):
            j = 2 * t + 1
            scores(j + 1, 0)
            softmax(j, 1, masked=False)
            accumulate(j - 1, 0)
            scores(j + 2, 1)
            softmax(j + 1, 0, masked=False)
            accumulate(j, 1)
            return carry

        lax.fori_loop(0, (n - 1) // 2, pair, 0)

        @pl.when(n % 2 == 0)
        def _():
            scores(n, 0)
            softmax(n - 1, 1, masked=False)
            accumulate(n - 2, 0)
            softmax(n, 0, masked=True)
            accumulate(n - 1, 1)
            accumulate(n, 0)

        @pl.when(n % 2 == 1)
        def _():
            softmax(n, 1, masked=True)
            accumulate(n - 1, 0)
            accumulate(n, 1)

    lf = lam_ref[...]
    lam = (jnp.exp(jnp.sum(lf[0:1] * lf[1:2], axis=-1, keepdims=True))
           - jnp.exp(jnp.sum(lf[2:3] * lf[3:4], axis=-1, keepdims=True)) + lambda_init)
    inv1 = 1.0 / jnp.sum(l_ref[0], axis=-1, keepdims=True)
    inv2 = 1.0 / jnp.sum(l_ref[1], axis=-1, keepdims=True)
    o = acc_ref[0] * inv1 - lam * (acc_ref[1] * inv2)
    y = o * lax.rsqrt(jnp.mean(o * o, axis=-1, keepdims=True) + LN_EPS)
    o_ref[...] = (y * g_ref[...] * (1.0 - lambda_init)).astype(BF16)


def _diff_attention(qkv, diff_lambda, subln_g, lambda_init, tq=512):
    S = qkv.shape[0]
    tq = min(tq, S // DIFF_KEY_TILES)
    tk = DIFF_KEY_TILES * tq
    qoff = 3 * RET_W // DIFF_DV
    koff = qoff + DIFF_W // DIFF_DV
    voff = koff + DIFF_W // DIFF_DV
    return pl.pallas_call(
        functools.partial(_diff_kernel, tq=tq, lambda_init=lambda_init),
        grid=(DIFF_HEADS, S // tq),
        in_specs=[pl.BlockSpec((4, DIFF_DK), lambda h, i: (0, 0)),
                  pl.BlockSpec((tq, DIFF_DV), lambda h, i: (i, qoff + h)),
                  pl.BlockSpec((S, DIFF_DV), lambda h, i: (0, koff + h), pipeline_mode=pl.Buffered(1)),
                  pl.BlockSpec((S, DIFF_DV), lambda h, i: (0, voff + h), pipeline_mode=pl.Buffered(1)),
                  pl.BlockSpec((1, DIFF_DV), lambda h, i: (0, h))],
        out_specs=pl.BlockSpec((tq, DIFF_DV), lambda h, i: (i, h)),
        out_shape=jax.ShapeDtypeStruct((S, DIFF_W), BF16),
        scratch_shapes=[pltpu.VMEM((tk, LANES), BF16),
                        pltpu.VMEM((2, 2, tq, tk), F32),
                        pltpu.VMEM((2, 2, tq, tk), BF16), pltpu.VMEM((2, 2, tq, LANES), F32),
                        pltpu.VMEM((2, tq, LANES), F32), pltpu.VMEM((2, tq, LANES), F32),
                        pltpu.VMEM((2, tq, DIFF_DV), F32)],
        compiler_params=_params("parallel", "arbitrary"),
    )(diff_lambda, qkv, qkv, qkv, subln_g.reshape(1, DIFF_W))


def _conv_kernel(cur_ref, halo_ref, w_ref, bdw_ref, g_ref, b_ref, o_ref, ext_ref, sh_ref, y_ref, *, tm):
    i = pl.program_id(0)
    ext_ref[0:CONV_HALO, :] = jnp.where(i > 0, halo_ref[...], 0.0)
    ext_ref[CONV_HALO:, :] = cur_ref[...]
    first = CONV_HALO - (CONV_WIDTH - 1)
    D = cur_ref.shape[1]
    nc = D // LANES
    for b in range(SUBLANES):
        n = tm + (CONV_WIDTH - 1 - b) // SUBLANES * SUBLANES
        for c in range(nc):
            sh_ref[b, c, 0:n, :] = ext_ref[first + b:first + b + n, c * LANES:(c + 1) * LANES]

    def channel_block(c, carry):
        acc = jnp.broadcast_to(bdw_ref[c], (tm, LANES))
        for j in range(CONV_WIDTH):
            b = j % SUBLANES
            acc = acc + w_ref[j, c] * sh_ref[b, c, j - b:j - b + tm, :]
        y_ref[c] = acc
        return carry

    lax.fori_loop(0, nc, channel_block, 0)

    total = y_ref[0]
    for c in range(1, nc):
        total = total + y_ref[c]
    mu = jnp.sum(total, axis=-1, keepdims=True) * (1.0 / D)
    sq = jnp.square(y_ref[0] - mu)
    for c in range(1, nc):
        sq = sq + jnp.square(y_ref[c] - mu)
    rstd = lax.rsqrt(jnp.sum(sq, axis=-1, keepdims=True) * (1.0 / D) + LN_EPS)
    for c in range(nc):
        cs = slice(c * LANES, (c + 1) * LANES)
        y = (y_ref[c] - mu) * rstd * g_ref[:, cs] + b_ref[:, cs]
        o_ref[:, cs] = _silu(y).astype(BF16)


def _conv_ln_silu(u, w_dw, b_dw, g, b, tm=128):
    S, D = u.shape
    per = tm // CONV_HALO
    nc = D // LANES
    fixed = lambda i: (0, 0)
    return pl.pallas_call(
        functools.partial(_conv_kernel, tm=tm),
        grid=(S // tm,),
        in_specs=[pl.BlockSpec((tm, D), lambda i: (i, 0)),
                  pl.BlockSpec((CONV_HALO, D), lambda i: (jnp.maximum(i * per - 1, 0), 0)),
                  pl.BlockSpec((CONV_WIDTH, nc, 1, LANES), lambda i: (0, 0, 0, 0)),
                  pl.BlockSpec((nc, 1, LANES), lambda i: (0, 0, 0)),
                  pl.BlockSpec((1, D), fixed), pl.BlockSpec((1, D), fixed)],
        out_specs=pl.BlockSpec((tm, D), lambda i: (i, 0)),
        out_shape=jax.ShapeDtypeStruct((S, D), BF16),
        scratch_shapes=[pltpu.VMEM((tm + CONV_HALO, D), F32),
                        pltpu.VMEM((SUBLANES, nc, tm + CONV_HALO - SUBLANES, LANES), F32),
                        pltpu.VMEM((nc, tm, LANES), F32)],
        compiler_params=_params("parallel"),
    )(u, u, w_dw.reshape(CONV_WIDTH, nc, 1, LANES), b_dw.reshape(nc, 1, LANES), g.reshape(1, D), b.reshape(1, D))


def _xattn_kernel(wq_ref, k_ref, v_ref, wo_ref, r_ref, g_ref, b_ref, of_ref, ob_ref):
    q = _dot(r_ref[...].astype(BF16), wq_ref[...]).astype(BF16)
    heads = []
    for h in range(XATTN_HEADS):
        cs = slice(h * XATTN_DH, (h + 1) * XATTN_DH)
        s = lax.dot_general(q[:, cs], k_ref[:, cs], _NT, preferred_element_type=F32)
        m = jnp.max(s, axis=-1, keepdims=True)
        p = jnp.exp2(s - m)
        p = p * (1.0 / jnp.sum(p, axis=-1, keepdims=True))
        heads.append(_dot(p.astype(BF16), v_ref[:, cs]).astype(BF16))
    o = jnp.concatenate(heads, axis=-1)
    z = ALPHA * r_ref[...] + _dot(o, wo_ref[...])
    y = _layer_norm(z, g_ref[...], b_ref[...])
    of_ref[...] = y
    ob_ref[...] = y.astype(BF16)


def _xattn(wq, k, v, wo, resid, g, b, tm=512):
    M, D = resid.shape
    row = lambda i: (i, 0)
    fixed = lambda i: (0, 0)
    resident = lambda shape: pl.BlockSpec(shape, fixed, pipeline_mode=pl.Buffered(1))
    return pl.pallas_call(
        _xattn_kernel,
        grid=(M // tm,),
        in_specs=[resident((D, D)), resident(k.shape), resident(v.shape), resident((D, D)),
                  pl.BlockSpec((tm, D), row),
                  pl.BlockSpec((1, D), fixed), pl.BlockSpec((1, D), fixed)],
        out_specs=[pl.BlockSpec((tm, D), row), pl.BlockSpec((tm, D), row)],
        out_shape=[jax.ShapeDtypeStruct((M, D), F32), jax.ShapeDtypeStruct((M, D), BF16)],
        compiler_params=_params("parallel"),
    )(wq, k, v, wo, resid, g.reshape(1, D), b.reshape(1, D))


def _lambda_init_for(layer):
    return 0.8 - 0.6 * math.exp(-0.3 * layer)


def kernel(x, mem, w_in, ret_gn_g, ret_gn_b, diff_lambda, diff_subln_g, w_mix_out, conv_w_pw1, conv_b_pw1, conv_w_dw, conv_b_dw, conv_ln_g, conv_ln_b, conv_w_pw2, conv_b_pw2, xattn_wq, xattn_wk, xattn_wv, xattn_wo, ffn_w_in, ffn_w_out, ln_g, ln_b):
    B, S, D = x.shape
    assert B == 1 and D == D_MODEL
    h = x[0]
    hb = h.astype(BF16)
    memb = mem[0].astype(BF16)
    zero_bias = jnp.zeros((D,), F32)
    for layer in range(DEPTH):
        last = layer == DEPTH - 1
        if layer % 2 == 0:
            e = layer // 2
            gate_blk, dq_blk = 3, 4
            col_scale = jnp.ones((1, w_in.shape[2]), F32).at[:, dq_blk * DIFF_W:(dq_blk + 1) * DIFF_W].set(
                DIFF_DK ** -0.5 * LOG2E)
            qkv = _matmul(hb, w_in, e, 6, BF16, src_block=lambda j: jnp.where(j >= gate_blk, j + 1, j),
                          col_scale=col_scale)
            gate = _matmul(hb, w_in, e, 1, F32, src_block=lambda j: j + gate_blk)
            ret = _retention(qkv, gate, ret_gn_g[e], ret_gn_b[e])
            dif = _diff_attention(qkv, diff_lambda[e], diff_subln_g[e], _lambda_init_for(layer + 1))
            wo = w_mix_out[e].astype(BF16)
            h, _ = _matmul_ln([ret, dif], [wo[:RET_W], wo[RET_W:]], zero_bias, h,
                              ln_g[layer, 0], ln_b[layer, 0], emit_bf16=False)
        else:
            o = layer // 2
            u = _matmul_glu(hb, conv_w_pw1[o].astype(BF16), conv_b_pw1[o])
            c = _conv_ln_silu(u, conv_w_dw[o], conv_b_dw[o], conv_ln_g[o], conv_ln_b[o])
            h, _ = _matmul_ln([c], [conv_w_pw2[o].astype(BF16)], conv_b_pw2[o], h,
                              ln_g[layer, 0], ln_b[layer, 0], emit_bf16=False)
        k = _matmul(memb, xattn_wk, layer, 2, BF16)
        v = _matmul(memb, xattn_wv, layer, 2, BF16)
        wq = (xattn_wq[layer] * (XATTN_DH ** -0.5 * LOG2E)).astype(BF16)
        h, hb = _xattn(wq, k, v, xattn_wo[layer].astype(BF16), h, ln_g[layer, 1], ln_b[layer, 1])
        h, hb = _ffn(hb, ffn_w_in[layer].astype(BF16), ffn_w_out[layer].astype(BF16), h,
                     ln_g[layer, 2], ln_b[layer, 2], emit_bf16=not last)
    return h[None]
```

```python
import functools
import math

import jax
import jax.numpy as jnp
from jax import lax
from jax.experimental import pallas as pl
from jax.experimental.pallas import tpu as pltpu

F32 = jnp.float32
BF16 = jnp.bfloat16

D_MODEL = 2048
DEPTH = 2
RET_HEADS = 8
RET_DK = 128
RET_CHUNK = 128
RET_W = 1024
DIFF_HEADS = 4
DIFF_DK = 128
DIFF_DV = 256
DIFF_W = 1024
DIFF_ROW_CHUNK = 32
DIFF_KEY_TILES = 2
CONV_WIDTH = 31
CONV_HALO = 32
XATTN_HEADS = 4
XATTN_DH = D_MODEL // XATTN_HEADS
D_FF = 5632
ALPHA = (2.0 * DEPTH) ** 0.25
LN_EPS = 1e-5
NEG_BIG = -1e30
LOG2E = math.log2(math.e)
LANES = 128
SUBLANES = 8

VMEM_LIMIT_BYTES = 56 * 1024 * 1024

_NT = (((1,), (1,)), ((), ()))
_TN = (((0,), (0,)), ((), ()))


def _params(*sem):
    return pltpu.CompilerParams(dimension_semantics=sem, vmem_limit_bytes=VMEM_LIMIT_BYTES)


def _dot(a, b):
    return jnp.dot(a, b, preferred_element_type=F32)


def _layer_norm(z, g, b):
    mu = jnp.mean(z, axis=-1, keepdims=True)
    d = z - mu
    var = jnp.mean(d * d, axis=-1, keepdims=True)
    return d * lax.rsqrt(var + LN_EPS) * g + b


def _silu(x):
    return x * jax.nn.sigmoid(x)


def _mm_kernel(x_ref, w_ref, *rest, scaled):
    o_ref, wb_ref = rest[-2:]

    @pl.when(pl.program_id(1) == 0)
    def _():
        w = w_ref[...]
        if scaled:
            w = w * rest[0][...]
        wb_ref[...] = w.astype(BF16)

    o_ref[...] = _dot(x_ref[...], wb_ref[...]).astype(o_ref.dtype)


def _matmul(x, w3, lead, n_blocks, out_dtype, src_block=lambda j: j, col_scale=None, tm=1024, tn=1024):
    M, K = x.shape
    tm = min(tm, M)
    scaled = col_scale is not None
    in_specs = [pl.BlockSpec((tm, K), lambda j, i: (i, 0)),
                pl.BlockSpec((None, K, tn), lambda j, i: (lead, 0, src_block(j)))]
    args = [x, w3]
    if scaled:
        in_specs.append(pl.BlockSpec((1, tn), lambda j, i: (0, src_block(j))))
        args.append(col_scale)
    return pl.pallas_call(
        functools.partial(_mm_kernel, scaled=scaled),
        grid=(n_blocks, M // tm),
        in_specs=in_specs,
        out_specs=pl.BlockSpec((tm, tn), lambda j, i: (i, j)),
        out_shape=jax.ShapeDtypeStruct((M, n_blocks * tn), out_dtype),
        scratch_shapes=[pltpu.VMEM((K, tn), BF16)],
        compiler_params=_params("parallel", "arbitrary"),
    )(*args)


def _mm_glu_kernel(x_ref, wa_ref, wb_ref, ba_ref, bb_ref, o_ref):
    x = x_ref[...]
    a = _dot(x, wa_ref[...]) + ba_ref[...]
    b = _dot(x, wb_ref[...]) + bb_ref[...]
    o_ref[...] = a * jax.nn.sigmoid(b)


def _matmul_glu(x, w, bias, tm=2048, tn=512):
    M, K = x.shape
    tm = min(tm, M)
    N = w.shape[1] // 2
    nb = N // tn
    bias2 = bias.reshape(1, 2 * N)
    return pl.pallas_call(
        _mm_glu_kernel,
        grid=(M // tm, nb),
        in_specs=[pl.BlockSpec((tm, K), lambda i, j: (i, 0)),
                  pl.BlockSpec((K, tn), lambda i, j: (0, j)),
                  pl.BlockSpec((K, tn), lambda i, j: (0, j + nb)),
                  pl.BlockSpec((1, tn), lambda i, j: (0, j)),
                  pl.BlockSpec((1, tn), lambda i, j: (0, j + nb))],
        out_specs=pl.BlockSpec((tm, tn), lambda i, j: (i, j)),
        out_shape=jax.ShapeDtypeStruct((M, N), F32),
        compiler_params=_params("parallel", "arbitrary"),
    )(x, w, w, bias2, bias2)


def _mm_ln_kernel(*refs, n_pairs, emit_bf16):
    xs = refs[:n_pairs]
    ws = refs[n_pairs:2 * n_pairs]
    bias_ref, r_ref, g_ref, b_ref = refs[2 * n_pairs:2 * n_pairs + 4]
    outs = refs[2 * n_pairs + 4:]
    acc = _dot(xs[0][...], ws[0][...])
    for x_ref, w_ref in zip(xs[1:], ws[1:]):
        acc = acc + _dot(x_ref[...], w_ref[...])
    z = ALPHA * r_ref[...] + acc + bias_ref[...]
    y = _layer_norm(z, g_ref[...], b_ref[...])
    outs[0][...] = y
    if emit_bf16:
        outs[1][...] = y.astype(BF16)


def _matmul_ln(xs, ws, bias, resid, g, b, emit_bf16=True, tm=512):
    M, N = resid.shape
    n = len(xs)
    row = lambda i: (i, 0)
    fixed = lambda i: (0, 0)
    in_specs = ([pl.BlockSpec((tm, x.shape[1]), row) for x in xs]
                + [pl.BlockSpec(w.shape, fixed) for w in ws]
                + [pl.BlockSpec((1, N), fixed), pl.BlockSpec((tm, N), row),
                   pl.BlockSpec((1, N), fixed), pl.BlockSpec((1, N), fixed)])
    out_specs = [pl.BlockSpec((tm, N), row)]
    out_shape = [jax.ShapeDtypeStruct((M, N), F32)]
    if emit_bf16:
        out_specs.append(pl.BlockSpec((tm, N), row))
        out_shape.append(jax.ShapeDtypeStruct((M, N), BF16))
    outs = pl.pallas_call(
        functools.partial(_mm_ln_kernel, n_pairs=n, emit_bf16=emit_bf16),
        grid=(M // tm,),
        in_specs=in_specs, out_specs=out_specs, out_shape=out_shape,
        compiler_params=_params("parallel"),
    )(*xs, *ws, bias.reshape(1, N), resid, g.reshape(1, N), b.reshape(1, N))
    return outs if emit_bf16 else (outs[0], None)


def _ffn_kernel(x_ref, wg_ref, wu_ref, wo_ref, r_ref, g_ref, b_ref, *rest, nrb, nf, ne, emit_bf16):
    outs, a_ref, acc_ref = rest[:-2], rest[-2], rest[-1]
    i = pl.program_id(0)
    f = pl.program_id(1)
    cur = i % 2
    te = acc_ref.shape[1] // ne
    active = i < nrb

    def up():
        x = x_ref[...]
        a_ref[...] = (_silu(_dot(x, wg_ref[...])) * _dot(x, wu_ref[...])).astype(BF16)

    def down():
        acc_ref[cur] += _dot(a_ref[...], wo_ref[...])

    def epilogue(e):
        z = ALPHA * r_ref[...] + acc_ref[1 - cur, e * te:(e + 1) * te, :]
        y = _layer_norm(z, g_ref[...], b_ref[...])
        outs[0][...] = y
        if emit_bf16:
            outs[1][...] = y.astype(BF16)

    for e in range(ne):
        pl.when(jnp.logical_and(i > 0, f == e + 1))(functools.partial(epilogue, e))

    @pl.when(jnp.logical_and(active, f == 0))
    def _():
        acc_ref[cur] = jnp.zeros(acc_ref.shape[1:], F32)
        up()

    @pl.when(jnp.logical_and(active, jnp.logical_and(f > 0, f < nf)))
    def _():
        down()
        up()

    @pl.when(jnp.logical_and(active, f == nf))
    def _():
        down()


def _ffn(xb, w_in, w_out, resid, g, b, emit_bf16=True, tm=1024, tf=512, ne=4):
    M, D = resid.shape
    tm = min(tm, M)
    nrb = M // tm
    nf = D_FF // tf
    te = tm // ne
    assert ne <= nf
    fixed = lambda i, f: (0, 0)
    chunk = lambda i, f: jnp.where(i < nrb, jnp.minimum(f, nf - 1), nf - 1)
    slab = lambda i, f: (jnp.where(i == 0, 0, (i - 1) * ne + jnp.clip(f - 1, 0, ne - 1)), 0)
    out_specs = [pl.BlockSpec((te, D), slab)]
    out_shape = [jax.ShapeDtypeStruct((M, D), F32)]
    if emit_bf16:
        out_specs.append(pl.BlockSpec((te, D), slab))
        out_shape.append(jax.ShapeDtypeStruct((M, D), BF16))
    outs = pl.pallas_call(
        functools.partial(_ffn_kernel, nrb=nrb, nf=nf, ne=ne, emit_bf16=emit_bf16),
        grid=(nrb + 1, nf + 1),
        in_specs=[pl.BlockSpec((tm, D), lambda i, f: (jnp.minimum(i, nrb - 1), 0)),
                  pl.BlockSpec((D, tf), lambda i, f: (0, chunk(i, f))),
                  pl.BlockSpec((D, tf), lambda i, f: (0, chunk(i, f) + nf)),
                  pl.BlockSpec((tf, D), lambda i, f: (jnp.where(i < nrb, jnp.clip(f - 1, 0, nf - 1), nf - 1), 0)),
                  pl.BlockSpec((te, D), slab),
                  pl.BlockSpec((1, D), fixed), pl.BlockSpec((1, D), fixed)],
        out_specs=out_specs, out_shape=out_shape,
        scratch_shapes=[pltpu.VMEM((tm, tf), BF16), pltpu.VMEM((2, tm, D), F32)],
        compiler_params=_params("arbitrary", "arbitrary"),
    )(xb, w_in, w_in, w_out, resid, g.reshape(1, D), b.reshape(1, D))
    return outs if emit_bf16 else (outs[0], None)


def _ret_kernel(q_ref, k_ref, v_ref, gate_ref, gg_ref, gb_ref, o_ref, state_ref, *, nch):
    C = RET_CHUNK

    @pl.when(pl.program_id(0) == 0)
    def _():
        state_ref[...] = jnp.zeros_like(state_ref)

    scale = RET_DK ** -0.5
    row = lax.broadcasted_iota(jnp.int32, (C, C), 0).astype(F32)
    col = lax.broadcasted_iota(jnp.int32, (C, C), 1).astype(F32)
    dist = row - col
    for h in range(RET_HEADS):
        log_gamma = math.log1p(-(2.0 ** (-5.0 - h)))
        decay = jnp.where(dist >= 0, jnp.exp(log_gamma * jnp.maximum(dist, 0.0)), 0.0) * scale
        xi = jnp.exp(log_gamma * (row + 1.0))
        zeta = jnp.exp(log_gamma * (C - 1.0 - row)) * scale
        chunk_decay = math.exp(log_gamma * C)
        cs = slice(h * RET_DK, (h + 1) * RET_DK)
        gg = gg_ref[:, cs]
        gb = gb_ref[:, cs]
        for c in range(nch):
            rs = slice(c * C, (c + 1) * C)
            q = q_ref[rs, cs]
            k = k_ref[rs, cs]
            v = v_ref[rs, cs]
            state = state_ref[h]
            scores = lax.dot_general(q, k, _NT, preferred_element_type=F32) * decay
            y = _dot(scores.astype(BF16), v) + xi * _dot(q, state.astype(BF16))
            kz = (k.astype(F32) * zeta).astype(BF16)
            kv = lax.dot_general(kz, v, _TN, preferred_element_type=F32)
            state_ref[h] = state * chunk_decay + kv
            mu = jnp.mean(y, axis=-1, keepdims=True)
            d = y - mu
            var = jnp.mean(d * d, axis=-1, keepdims=True)
            yn = d * lax.rsqrt(var + LN_EPS) * gg + gb
            o_ref[rs, cs] = (_silu(gate_ref[rs, cs]) * yn).astype(BF16)


def _retention(qkv, gate, gn_g, gn_b, tr=512):
    S = qkv.shape[0]
    tr = min(tr, S)
    fixed = lambda i: (0, 0)
    return pl.pallas_call(
        functools.partial(_ret_kernel, nch=tr // RET_CHUNK),
        grid=(S // tr,),
        in_specs=[pl.BlockSpec((tr, RET_W), lambda i: (i, 0)),
                  pl.BlockSpec((tr, RET_W), lambda i: (i, 1)),
                  pl.BlockSpec((tr, RET_W), lambda i: (i, 2)),
                  pl.BlockSpec((tr, RET_W), lambda i: (i, 0)),
                  pl.BlockSpec((1, RET_W), fixed), pl.BlockSpec((1, RET_W), fixed)],
        out_specs=pl.BlockSpec((tr, RET_W), lambda i: (i, 0)),
        out_shape=jax.ShapeDtypeStruct((S, RET_W), BF16),
        scratch_shapes=[pltpu.VMEM((RET_HEADS, RET_DK, RET_DK), F32)],
        compiler_params=_params("arbitrary"),
    )(qkv, qkv, qkv, gate, gn_g.reshape(1, RET_W), gn_b.reshape(1, RET_W))


def _diff_kernel(lam_ref, q_ref, k_ref, v_ref, g_ref, o_ref, kb_ref, s_ref, p_ref, alpha_ref,
                 m_ref, l_ref, acc_ref, *, tq, lambda_init):
    h = pl.program_id(0)
    i = pl.program_id(1)
    tk = DIFF_KEY_TILES * tq
    n = i // DIFF_KEY_TILES
    m_ref[...] = jnp.full(m_ref.shape, NEG_BIG, F32)
    l_ref[...] = jnp.zeros_like(l_ref)
    acc_ref[...] = jnp.zeros_like(acc_ref)

    head = (h + 1).astype(F32)
    slope = jnp.exp2(jnp.full((tk, LANES), -8.0 / DIFF_HEADS, F32) * head) * LOG2E
    lane = lax.broadcasted_iota(jnp.int32, (tk, LANES), 1)

    def split3(x, first_lane, lanes):
        hi = x.astype(BF16).astype(F32)
        mid = (x - hi).astype(BF16).astype(F32)
        lo = x - hi - mid
        pieces = jnp.where(lanes == first_lane, hi,
                           jnp.where(lanes == first_lane + 1, mid,
                                     jnp.where(lanes == first_lane + 2, lo, 0.0)))
        return pieces.astype(BF16)

    @pl.when(i == 0)
    def _():
        within = slope * lax.broadcasted_iota(jnp.int32, (tk, LANES), 0).astype(F32)
        kb_ref[...] = jnp.where(jnp.logical_and(lane >= 3, lane < 6), 1.0,
                                split3(within, 0, lane).astype(F32)).astype(BF16)

    q_ones = jnp.where(lane[:SUBLANES] < 3, 1.0, 0.0)

    def scores(j, par):
        ks = k_ref[pl.ds(pl.multiple_of(j * tk, tk), tk), :]
        tile_bias = split3(slope[:SUBLANES] * (j * tk - i * tq).astype(F32), 3, lane[:SUBLANES]).astype(F32)
        qb = jnp.tile(q_ones + tile_bias, (tq // SUBLANES, 1)).astype(BF16)
        kb = kb_ref[...]
        for mi in range(2):
            ms = slice(mi * DIFF_DK, (mi + 1) * DIFF_DK)
            qa = jnp.concatenate([q_ref[:, ms], qb], axis=1)
            ka = jnp.concatenate([ks[:, ms], kb], axis=1)
            s_ref[par, mi] = lax.dot_general(qa, ka, _NT, preferred_element_type=F32)

    def softmax(j, par, masked):
        for mi in range(2):
            for r0 in range(0, tq, DIFF_ROW_CHUNK):
                rs = slice(r0, r0 + DIFF_ROW_CHUNK)
                def load(cs, width):
                    s = s_ref[par, mi, rs, cs]
                    if masked:
                        r = lax.broadcasted_iota(jnp.int32, (DIFF_ROW_CHUNK, width), 0) + (r0 + i * tq)
                        c = lax.broadcasted_iota(jnp.int32, (DIFF_ROW_CHUNK, width), 1) + (j * tk + cs.start)
                        s = jnp.where(r >= c, s, NEG_BIG)
                    return s

                m_old = m_ref[mi, rs, :]
                m_new = jnp.maximum(m_old, jnp.max(load(slice(0, tk), tk), axis=-1, keepdims=True))
                alpha = jnp.exp2(m_old - m_new)
                psum = None
                for t in range(tk // LANES):
                    cs = slice(t * LANES, (t + 1) * LANES)
                    p = jnp.exp2(load(cs, LANES) - m_new)
                    psum = p if psum is None else psum + p
                    p_ref[par, mi, rs, cs] = p.astype(BF16)
                l_ref[mi, rs, :] = alpha * l_ref[mi, rs, :] + psum
                m_ref[mi, rs, :] = m_new
                alpha_ref[par, mi, rs, :] = alpha

    def accumulate(j, par):
        vs = v_ref[pl.ds(pl.multiple_of(j * tk, tk), tk), :]
        for mi in range(2):
            acc_ref[mi] = (jnp.tile(alpha_ref[par, mi], (1, DIFF_DV // LANES)) * acc_ref[mi]
                           + _dot(p_ref[par, mi], vs))

    scores(0, 0)

    @pl.when(n == 0)
    def _():
        softmax(0, 0, masked=True)
        accumulate(0, 0)

    @pl.when(n > 0)
    def _():
        scores(1, 1)
        softmax(0, 0, masked=False)

        def pair(t, carry):
            j = 2 * t + 1
            scores(j + 1, 0)
            softmax(j, 1, masked=False)
            accumulate(j - 1, 0)
            scores(j + 2, 1)
            softmax(j + 1, 0, masked=False)
            accumulate(j, 1)
            return carry

        lax.fori_loop(0, (n - 1) // 2, pair, 0)

        @pl.when(n % 2 == 0)
        def _():
            scores(n, 0)
            softmax(n - 1, 1, masked=False)
            accumulate(n - 2, 0)
            softmax(n, 0, masked=True)
            accumulate(n - 1, 1)
            accumulate(n, 0)

        @pl.when(n % 2 == 1)
        def _():
            softmax(n, 1, masked=True)
            accumulate(n - 1, 0)
            accumulate(n, 1)

    lf = lam_ref[...]
    lam = (jnp.exp(jnp.sum(lf[0:1] * lf[1:2], axis=-1, keepdims=True))
           - jnp.exp(jnp.sum(lf[2:3] * lf[3:4], axis=-1, keepdims=True)) + lambda_init)
    inv1 = 1.0 / jnp.sum(l_ref[0], axis=-1, keepdims=True)
    inv2 = 1.0 / jnp.sum(l_ref[1], axis=-1, keepdims=True)
    o = acc_ref[0] * inv1 - lam * (acc_ref[1] * inv2)
    y = o * lax.rsqrt(jnp.mean(o * o, axis=-1, keepdims=True) + LN_EPS)
    o_ref[...] = (y * g_ref[...] * (1.0 - lambda_init)).astype(BF16)


def _diff_attention(qkv, diff_lambda, subln_g, lambda_init, tq=512):
    S = qkv.shape[0]
    tq = min(tq, S // DIFF_KEY_TILES)
    tk = DIFF_KEY_TILES * tq
    qoff = 3 * RET_W // DIFF_DV
    koff = qoff + DIFF_W // DIFF_DV
    voff = koff + DIFF_W // DIFF_DV
    return pl.pallas_call(
        functools.partial(_diff_kernel, tq=tq, lambda_init=lambda_init),
        grid=(DIFF_HEADS, S // tq),
        in_specs=[pl.BlockSpec((4, DIFF_DK), lambda h, i: (0, 0)),
                  pl.BlockSpec((tq, DIFF_DV), lambda h, i: (i, qoff + h)),
                  pl.BlockSpec((S, DIFF_DV), lambda h, i: (0, koff + h), pipeline_mode=pl.Buffered(1)),
                  pl.BlockSpec((S, DIFF_DV), lambda h, i: (0, voff + h), pipeline_mode=pl.Buffered(1)),
                  pl.BlockSpec((1, DIFF_DV), lambda h, i: (0, h))],
        out_specs=pl.BlockSpec((tq, DIFF_DV), lambda h, i: (i, h)),
        out_shape=jax.ShapeDtypeStruct((S, DIFF_W), BF16),
        scratch_shapes=[pltpu.VMEM((tk, LANES), BF16),
                        pltpu.VMEM((2, 2, tq, tk), F32),
                        pltpu.VMEM((2, 2, tq, tk), BF16), pltpu.VMEM((2, 2, tq, LANES), F32),
                        pltpu.VMEM((2, tq, LANES), F32), pltpu.VMEM((2, tq, LANES), F32),
                        pltpu.VMEM((2, tq, DIFF_DV), F32)],
        compiler_params=_params("parallel", "arbitrary"),
    )(diff_lambda, qkv, qkv, qkv, subln_g.reshape(1, DIFF_W))


def _conv_kernel(cur_ref, halo_ref, w_ref, bdw_ref, g_ref, b_ref, o_ref, ext_ref, sh_ref, y_ref, *, tm):
    i = pl.program_id(0)
    ext_ref[0:CONV_HALO, :] = jnp.where(i > 0, halo_ref[...], 0.0)
    ext_ref[CONV_HALO:, :] = cur_ref[...]
    first = CONV_HALO - (CONV_WIDTH - 1)
    D = cur_ref.shape[1]
    nc = D // LANES
    for b in range(SUBLANES):
        n = tm + (CONV_WIDTH - 1 - b) // SUBLANES * SUBLANES
        for c in range(nc):
            sh_ref[b, c, 0:n, :] = ext_ref[first + b:first + b + n, c * LANES:(c + 1) * LANES]

    def channel_block(c, carry):
        acc = jnp.broadcast_to(bdw_ref[c], (tm, LANES))
        for j in range(CONV_WIDTH):
            b = j % SUBLANES
            acc = acc + w_ref[j, c] * sh_ref[b, c, j - b:j - b + tm, :]
        y_ref[c] = acc
        return carry

    lax.fori_loop(0, nc, channel_block, 0)

    total = y_ref[0]
    for c in range(1, nc):
        total = total + y_ref[c]
    mu = jnp.sum(total, axis=-1, keepdims=True) * (1.0 / D)
    sq = jnp.square(y_ref[0] - mu)
    for c in range(1, nc):
        sq = sq + jnp.square(y_ref[c] - mu)
    rstd = lax.rsqrt(jnp.sum(sq, axis=-1, keepdims=True) * (1.0 / D) + LN_EPS)
    for c in range(nc):
        cs = slice(c * LANES, (c + 1) * LANES)
        y = (y_ref[c] - mu) * rstd * g_ref[:, cs] + b_ref[:, cs]
        o_ref[:, cs] = _silu(y).astype(BF16)


def _conv_ln_silu(u, w_dw, b_dw, g, b, tm=128):
    S, D = u.shape
    per = tm // CONV_HALO
    nc = D // LANES
    fixed = lambda i: (0, 0)
    return pl.pallas_call(
        functools.partial(_conv_kernel, tm=tm),
        grid=(S // tm,),
        in_specs=[pl.BlockSpec((tm, D), lambda i: (i, 0)),
                  pl.BlockSpec((CONV_HALO, D), lambda i: (jnp.maximum(i * per - 1, 0), 0)),
                  pl.BlockSpec((CONV_WIDTH, nc, 1, LANES), lambda i: (0, 0, 0, 0)),
                  pl.BlockSpec((nc, 1, LANES), lambda i: (0, 0, 0)),
                  pl.BlockSpec((1, D), fixed), pl.BlockSpec((1, D), fixed)],
        out_specs=pl.BlockSpec((tm, D), lambda i: (i, 0)),
        out_shape=jax.ShapeDtypeStruct((S, D), BF16),
        scratch_shapes=[pltpu.VMEM((tm + CONV_HALO, D), F32),
                        pltpu.VMEM((SUBLANES, nc, tm + CONV_HALO - SUBLANES, LANES), F32),
                        pltpu.VMEM((nc, tm, LANES), F32)],
        compiler_params=_params("parallel"),
    )(u, u, w_dw.reshape(CONV_WIDTH, nc, 1, LANES), b_dw.reshape(nc, 1, LANES), g.reshape(1, D), b.reshape(1, D))


def _xattn_kernel(wq_ref, k_ref, v_ref, wo_ref, r_ref, g_ref, b_ref, of_ref, ob_ref):
    q = _dot(r_ref[...].astype(BF16), wq_ref[...]).astype(BF16)
    heads = []
    for h in range(XATTN_HEADS):
        cs = slice(h * XATTN_DH, (h + 1) * XATTN_DH)
        s = lax.dot_general(q[:, cs], k_ref[:, cs], _NT, preferred_element_type=F32)
        m = jnp.max(s, axis=-1, keepdims=True)
        p = jnp.exp2(s - m)
        p = p * (1.0 / jnp.sum(p, axis=-1, keepdims=True))
        heads.append(_dot(p.astype(BF16), v_ref[:, cs]).astype(BF16))
    o = jnp.concatenate(heads, axis=-1)
    z = ALPHA * r_ref[...] + _dot(o, wo_ref[...])
    y = _layer_norm(z, g_ref[...], b_ref[...])
    of_ref[...] = y
    ob_ref[...] = y.astype(BF16)


def _xattn(wq, k, v, wo, resid, g, b, tm=512):
    M, D = resid.shape
    row = lambda i: (i, 0)
    fixed = lambda i: (0, 0)
    resident = lambda shape: pl.BlockSpec(shape, fixed, pipeline_mode=pl.Buffered(1))
    return pl.pallas_call(
        _xattn_kernel,
        grid=(M // tm,),
        in_specs=[resident((D, D)), resident(k.shape), resident(v.shape), resident((D, D)),
                  pl.BlockSpec((tm, D), row),
                  pl.BlockSpec((1, D), fixed), pl.BlockSpec((1, D), fixed)],
        out_specs=[pl.BlockSpec((tm, D), row), pl.BlockSpec((tm, D), row)],
        out_shape=[jax.ShapeDtypeStruct((M, D), F32), jax.ShapeDtypeStruct((M, D), BF16)],
        compiler_params=_params("parallel"),
    )(wq, k, v, wo, resid, g.reshape(1, D), b.reshape(1, D))


def _lambda_init_for(layer):
    return 0.8 - 0.6 * math.exp(-0.3 * layer)


def kernel(x, mem, w_in, ret_gn_g, ret_gn_b, diff_lambda, diff_subln_g, w_mix_out, conv_w_pw1, conv_b_pw1, conv_w_dw, conv_b_dw, conv_ln_g, conv_ln_b, conv_w_pw2, conv_b_pw2, xattn_wq, xattn_wk, xattn_wv, xattn_wo, ffn_w_in, ffn_w_out, ln_g, ln_b):
    B, S, D = x.shape
    assert B == 1 and D == D_MODEL
    h = x[0]
    hb = h.astype(BF16)
    memb = mem[0].astype(BF16)
    zero_bias = jnp.zeros((D,), F32)
    for layer in range(DEPTH):
        last = layer == DEPTH - 1
        if layer % 2 == 0:
            e = layer // 2
            gate_blk, dq_blk = 3, 4
            col_scale = jnp.ones((1, w_in.shape[2]), F32).at[:, dq_blk * DIFF_W:(dq_blk + 1) * DIFF_W].set(
                DIFF_DK ** -0.5 * LOG2E)
            qkv = _matmul(hb, w_in, e, 6, BF16, src_block=lambda j: jnp.where(j >= gate_blk, j + 1, j),
                          col_scale=col_scale)
            gate = _matmul(hb, w_in, e, 1, F32, src_block=lambda j: j + gate_blk)
            ret = _retention(qkv, gate, ret_gn_g[e], ret_gn_b[e])
            dif = _diff_attention(qkv, diff_lambda[e], diff_subln_g[e], _lambda_init_for(layer + 1))
            wo = w_mix_out[e].astype(BF16)
            h, _ = _matmul_ln([ret, dif], [wo[:RET_W], wo[RET_W:]], zero_bias, h,
                              ln_g[layer, 0], ln_b[layer, 0], emit_bf16=False)
        else:
            o = layer // 2
            u = _matmul_glu(hb, conv_w_pw1[o].astype(BF16), conv_b_pw1[o])
            c = _conv_ln_silu(u, conv_w_dw[o], conv_b_dw[o], conv_ln_g[o], conv_ln_b[o])
            h, _ = _matmul_ln([c], [conv_w_pw2[o].astype(BF16)], conv_b_pw2[o], h,
                              ln_g[layer, 0], ln_b[layer, 0], emit_bf16=False)
        k = _matmul(memb, xattn_wk, layer, 2, BF16)
        v = _matmul(memb, xattn_wv, layer, 2, BF16)
        wq = (xattn_wq[layer] * (XATTN_DH ** -0.5 * LOG2E)).astype(BF16)
        h, hb = _xattn(wq, k, v, xattn_wo[layer].astype(BF16), h, ln_g[layer, 1], ln_b[layer, 1])
        h, hb = _ffn(hb, ffn_w_in[layer].astype(BF16), ffn_w_out[layer].astype(BF16), h,
                     ln_g[layer, 2], ln_b[layer, 2], emit_bf16=not last)
    return h[None]
```

```python
import functools
import math

import jax
import jax.numpy as jnp
from jax import lax
from jax.experimental import pallas as pl
from jax.experimental.pallas import tpu as pltpu

F32 = jnp.float32
BF16 = jnp.bfloat16

D_MODEL = 2048
DEPTH = 2
RET_HEADS = 8
RET_DK = 128
RET_CHUNK = 128
RET_W = 1024
DIFF_HEADS = 4
DIFF_DK = 128
DIFF_DV = 256
DIFF_W = 1024
DIFF_ROW_CHUNK = 32
DIFF_KEY_TILES = 2
CONV_WIDTH = 31
CONV_HALO = 32
XATTN_HEADS = 4
XATTN_DH = D_MODEL // XATTN_HEADS
D_FF = 5632
ALPHA = (2.0 * DEPTH) ** 0.25
LN_EPS = 1e-5
NEG_BIG = -1e30
LOG2E = math.log2(math.e)
LANES = 128
SUBLANES = 8

VMEM_LIMIT_BYTES = 56 * 1024 * 1024

_NT = (((1,), (1,)), ((), ()))
_TN = (((0,), (0,)), ((), ()))


def _params(*sem):
    return pltpu.CompilerParams(dimension_semantics=sem, vmem_limit_bytes=VMEM_LIMIT_BYTES)


def _dot(a, b):
    return jnp.dot(a, b, preferred_element_type=F32)


def _layer_norm(z, g, b):
    mu = jnp.mean(z, axis=-1, keepdims=True)
    d = z - mu
    var = jnp.mean(d * d, axis=-1, keepdims=True)
    return d * lax.rsqrt(var + LN_EPS) * g + b


def _silu(x):
    return x * jax.nn.sigmoid(x)


def _mm_kernel(x_ref, w_ref, *rest, scaled):
    o_ref, wb_ref = rest[-2:]

    @pl.when(pl.program_id(1) == 0)
    def _():
        w = w_ref[...]
        if scaled:
            w = w * rest[0][...]
        wb_ref[...] = w.astype(BF16)

    o_ref[...] = _dot(x_ref[...], wb_ref[...]).astype(o_ref.dtype)


def _matmul(x, w3, lead, n_blocks, out_dtype, src_block=lambda j: j, col_scale=None, tm=1024, tn=1024):
    M, K = x.shape
    tm = min(tm, M)
    scaled = col_scale is not None
    in_specs = [pl.BlockSpec((tm, K), lambda j, i: (i, 0)),
                pl.BlockSpec((None, K, tn), lambda j, i: (lead, 0, src_block(j)))]
    args = [x, w3]
    if scaled:
        in_specs.append(pl.BlockSpec((1, tn), lambda j, i: (0, src_block(j))))
        args.append(col_scale)
    return pl.pallas_call(
        functools.partial(_mm_kernel, scaled=scaled),
        grid=(n_blocks, M // tm),
        in_specs=in_specs,
        out_specs=pl.BlockSpec((tm, tn), lambda j, i: (i, j)),
        out_shape=jax.ShapeDtypeStruct((M, n_blocks * tn), out_dtype),
        scratch_shapes=[pltpu.VMEM((K, tn), BF16)],
        compiler_params=_params("parallel", "arbitrary"),
    )(*args)


def _mm_glu_kernel(x_ref, wa_ref, wb_ref, ba_ref, bb_ref, o_ref):
    x = x_ref[...]
    a = _dot(x, wa_ref[...]) + ba_ref[...]
    b = _dot(x, wb_ref[...]) + bb_ref[...]
    o_ref[...] = a * jax.nn.sigmoid(b)


def _matmul_glu(x, w, bias, tm=2048, tn=512):
    M, K = x.shape
    tm = min(tm, M)
    N = w.shape[1] // 2
    nb = N // tn
    bias2 = bias.reshape(1, 2 * N)
    return pl.pallas_call(
        _mm_glu_kernel,
        grid=(M // tm, nb),
        in_specs=[pl.BlockSpec((tm, K), lambda i, j: (i, 0)),
                  pl.BlockSpec((K, tn), lambda i, j: (0, j)),
                  pl.BlockSpec((K, tn), lambda i, j: (0, j + nb)),
                  pl.BlockSpec((1, tn), lambda i, j: (0, j)),
                  pl.BlockSpec((1, tn), lambda i, j: (0, j + nb))],
        out_specs=pl.BlockSpec((tm, tn), lambda i, j: (i, j)),
        out_shape=jax.ShapeDtypeStruct((M, N), F32),
        compiler_params=_params("parallel", "arbitrary"),
    )(x, w, w, bias2, bias2)


def _mm_ln_kernel(*refs, n_pairs, emit_bf16):
    xs = refs[:n_pairs]
    ws = refs[n_pairs:2 * n_pairs]
    bias_ref, r_ref, g_ref, b_ref = refs[2 * n_pairs:2 * n_pairs + 4]
    outs = refs[2 * n_pairs + 4:]
    acc = _dot(xs[0][...], ws[0][...])
    for x_ref, w_ref in zip(xs[1:], ws[1:]):
        acc = acc + _dot(x_ref[...], w_ref[...])
    z = ALPHA * r_ref[...] + acc + bias_ref[...]
    y = _layer_norm(z, g_ref[...], b_ref[...])
    outs[0][...] = y
    if emit_bf16:
        outs[1][...] = y.astype(BF16)


def _matmul_ln(xs, ws, bias, resid, g, b, emit_bf16=True, tm=512):
    M, N = resid.shape
    n = len(xs)
    row = lambda i: (i, 0)
    fixed = lambda i: (0, 0)
    in_specs = ([pl.BlockSpec((tm, x.shape[1]), row) for x in xs]
                + [pl.BlockSpec(w.shape, fixed) for w in ws]
                + [pl.BlockSpec((1, N), fixed), pl.BlockSpec((tm, N), row),
                   pl.BlockSpec((1, N), fixed), pl.BlockSpec((1, N), fixed)])
    out_specs = [pl.BlockSpec((tm, N), row)]
    out_shape = [jax.ShapeDtypeStruct((M, N), F32)]
    if emit_bf16:
        out_specs.append(pl.BlockSpec((tm, N), row))
        out_shape.append(jax.ShapeDtypeStruct((M, N), BF16))
    outs = pl.pallas_call(
        functools.partial(_mm_ln_kernel, n_pairs=n, emit_bf16=emit_bf16),
        grid=(M // tm,),
        in_specs=in_specs, out_specs=out_specs, out_shape=out_shape,
        compiler_params=_params("parallel"),
    )(*xs, *ws, bias.reshape(1, N), resid, g.reshape(1, N), b.reshape(1, N))
    return outs if emit_bf16 else (outs[0], None)


def _ffn_kernel(x_ref, wg_ref, wu_ref, wo_ref, r_ref, g_ref, b_ref, *rest, nrb, nf, ne, emit_bf16):
    outs, a_ref, acc_ref = rest[:-2], rest[-2], rest[-1]
    i = pl.program_id(0)
    f = pl.program_id(1)
    cur = i % 2
    te = acc_ref.shape[1] // ne
    active = i < nrb

    def up():
        x = x_ref[...]
        a_ref[...] = (_silu(_dot(x, wg_ref[...])) * _dot(x, wu_ref[...])).astype(BF16)

    def down():
        acc_ref[cur] += _dot(a_ref[...], wo_ref[...])

    def epilogue(e):
        z = ALPHA * r_ref[...] + acc_ref[1 - cur, e * te:(e + 1) * te, :]
        y = _layer_norm(z, g_ref[...], b_ref[...])
        outs[0][...] = y
        if emit_bf16:
            outs[1][...] = y.astype(BF16)

    for e in range(ne):
        pl.when(jnp.logical_and(i > 0, f == e + 1))(functools.partial(epilogue, e))

    @pl.when(jnp.logical_and(active, f == 0))
    def _():
        acc_ref[cur] = jnp.zeros(acc_ref.shape[1:], F32)
        up()

    @pl.when(jnp.logical_and(active, jnp.logical_and(f > 0, f < nf)))
    def _():
        down()
        up()

    @pl.when(jnp.logical_and(active, f == nf))
    def _():
        down()


def _ffn(xb, w_in, w_out, resid, g, b, emit_bf16=True, tm=1024, tf=512, ne=4):
    M, D = resid.shape
    tm = min(tm, M)
    nrb = M // tm
    nf = D_FF // tf
    te = tm // ne
    assert ne <= nf
    fixed = lambda i, f: (0, 0)
    chunk = lambda i, f: jnp.where(i < nrb, jnp.minimum(f, nf - 1), nf - 1)
    slab = lambda i, f: (jnp.where(i == 0, 0, (i - 1) * ne + jnp.clip(f - 1, 0, ne - 1)), 0)
    out_specs = [pl.BlockSpec((te, D), slab)]
    out_shape = [jax.ShapeDtypeStruct((M, D), F32)]
    if emit_bf16:
        out_specs.append(pl.BlockSpec((te, D), slab))
        out_shape.append(jax.ShapeDtypeStruct((M, D), BF16))
    outs = pl.pallas_call(
        functools.partial(_ffn_kernel, nrb=nrb, nf=nf, ne=ne, emit_bf16=emit_bf16),
        grid=(nrb + 1, nf + 1),
        in_specs=[pl.BlockSpec((tm, D), lambda i, f: (jnp.minimum(i, nrb - 1), 0)),
                  pl.BlockSpec((D, tf), lambda i, f: (0, chunk(i, f))),
                  pl.BlockSpec((D, tf), lambda i, f: (0, chunk(i, f) + nf)),
                  pl.BlockSpec((tf, D), lambda i, f: (jnp.where(i < nrb, jnp.clip(f - 1, 0, nf - 1), nf - 1), 0)),
                  pl.BlockSpec((te, D), slab),
                  pl.BlockSpec((1, D), fixed), pl.BlockSpec((1, D), fixed)],
        out_specs=out_specs, out_shape=out_shape,
        scratch_shapes=[pltpu.VMEM((tm, tf), BF16), pltpu.VMEM((2, tm, D), F32)],
        compiler_params=_params("arbitrary", "arbitrary"),
    )(xb, w_in, w_in, w_out, resid, g.reshape(1, D), b.reshape(1, D))
    return outs if emit_bf16 else (outs[0], None)


def _ret_kernel(q_ref, k_ref, v_ref, gate_ref, gg_ref, gb_ref, o_ref, state_ref, *, nch):
    C = RET_CHUNK

    @pl.when(pl.program_id(0) == 0)
    def _():
        state_ref[...] = jnp.zeros_like(state_ref)

    scale = RET_DK ** -0.5
    row = lax.broadcasted_iota(jnp.int32, (C, C), 0).astype(F32)
    col = lax.broadcasted_iota(jnp.int32, (C, C), 1).astype(F32)
    dist = row - col
    for h in range(RET_HEADS):
        log_gamma = math.log1p(-(2.0 ** (-5.0 - h)))
        decay = jnp.where(dist >= 0, jnp.exp(log_gamma * jnp.maximum(dist, 0.0)), 0.0) * scale
        xi = jnp.exp(log_gamma * (row + 1.0))
        zeta = jnp.exp(log_gamma * (C - 1.0 - row)) * scale
        chunk_decay = math.exp(log_gamma * C)
        cs = slice(h * RET_DK, (h + 1) * RET_DK)
        gg = gg_ref[:, cs]
        gb = gb_ref[:, cs]
        for c in range(nch):
            rs = slice(c * C, (c + 1) * C)
            q = q_ref[rs, cs]
            k = k_ref[rs, cs]
            v = v_ref[rs, cs]
            state = state_ref[h]
            scores = lax.dot_general(q, k, _NT, preferred_element_type=F32) * decay
            y = _dot(scores.astype(BF16), v) + xi * _dot(q, state.astype(BF16))
            kz = (k.astype(F32) * zeta).astype(BF16)
            kv = lax.dot_general(kz, v, _TN, preferred_element_type=F32)
            state_ref[h] = state * chunk_decay + kv
            mu = jnp.mean(y, axis=-1, keepdims=True)
            d = y - mu
            var = jnp.mean(d * d, axis=-1, keepdims=True)
            yn = d * lax.rsqrt(var + LN_EPS) * gg + gb
            o_ref[rs, cs] = (_silu(gate_ref[rs, cs]) * yn).astype(BF16)


def _retention(qkv, gate, gn_g, gn_b, tr=512):
    S = qkv.shape[0]
    tr = min(tr, S)
    fixed = lambda i: (0, 0)
    return pl.pallas_call(
        functools.partial(_ret_kernel, nch=tr // RET_CHUNK),
        grid=(S // tr,),
        in_specs=[pl.BlockSpec((tr, RET_W), lambda i: (i, 0)),
                  pl.BlockSpec((tr, RET_W), lambda i: (i, 1)),
                  pl.BlockSpec((tr, RET_W), lambda i: (i, 2)),
                  pl.BlockSpec((tr, RET_W), lambda i: (i, 0)),
                  pl.BlockSpec((1, RET_W), fixed), pl.BlockSpec((1, RET_W), fixed)],
        out_specs=pl.BlockSpec((tr, RET_W), lambda i: (i, 0)),
        out_shape=jax.ShapeDtypeStruct((S, RET_W), BF16),
        scratch_shapes=[pltpu.VMEM((RET_HEADS, RET_DK, RET_DK), F32)],
        compiler_params=_params("arbitrary"),
    )(qkv, qkv, qkv, gate, gn_g.reshape(1, RET_W), gn_b.reshape(1, RET_W))


def _diff_kernel(lam_ref, q_ref, k_ref, v_ref, g_ref, o_ref, kb_ref, s_ref, p_ref, alpha_ref,
                 m_ref, l_ref, acc_ref, *, tq, lambda_init):
    h = pl.program_id(0)
    i = pl.program_id(1)
    tk = DIFF_KEY_TILES * tq
    n = i // DIFF_KEY_TILES
    m_ref[...] = jnp.full(m_ref.shape, NEG_BIG, F32)
    l_ref[...] = jnp.zeros_like(l_ref)
    acc_ref[...] = jnp.zeros_like(acc_ref)

    head = (h + 1).astype(F32)
    slope = jnp.exp2(jnp.full((tk, LANES), -8.0 / DIFF_HEADS, F32) * head) * LOG2E
    lane = lax.broadcasted_iota(jnp.int32, (tk, LANES), 1)

    def split3(x, first_lane, lanes):
        hi = x.astype(BF16).astype(F32)
        mid = (x - hi).astype(BF16).astype(F32)
        lo = x - hi - mid
        pieces = jnp.where(lanes == first_lane, hi,
                           jnp.where(lanes == first_lane + 1, mid,
                                     jnp.where(lanes == first_lane + 2, lo, 0.0)))
        return pieces.astype(BF16)

    @pl.when(i == 0)
    def _():
        within = slope * lax.broadcasted_iota(jnp.int32, (tk, LANES), 0).astype(F32)
        kb_ref[...] = jnp.where(jnp.logical_and(lane >= 3, lane < 6), 1.0,
                                split3(within, 0, lane).astype(F32)).astype(BF16)

    q_ones = jnp.where(lane[:SUBLANES] < 3, 1.0, 0.0)

    def scores(j, par):
        ks = k_ref[pl.ds(pl.multiple_of(j * tk, tk), tk), :]
        tile_bias = split3(slope[:SUBLANES] * (j * tk - i * tq).astype(F32), 3, lane[:SUBLANES]).astype(F32)
        qb = jnp.tile(q_ones + tile_bias, (tq // SUBLANES, 1)).astype(BF16)
        kb = kb_ref[...]
        for mi in range(2):
            ms = slice(mi * DIFF_DK, (mi + 1) * DIFF_DK)
            qa = jnp.concatenate([q_ref[:, ms], qb], axis=1)
            ka = jnp.concatenate([ks[:, ms], kb], axis=1)
            s_ref[par, mi] = lax.dot_general(qa, ka, _NT, preferred_element_type=F32)

    def softmax(j, par, masked):
        for mi in range(2):
            for r0 in range(0, tq, DIFF_ROW_CHUNK):
                rs = slice(r0, r0 + DIFF_ROW_CHUNK)
                def load(cs, width):
                    s = s_ref[par, mi, rs, cs]
                    if masked:
                        r = lax.broadcasted_iota(jnp.int32, (DIFF_ROW_CHUNK, width), 0) + (r0 + i * tq)
                        c = lax.broadcasted_iota(jnp.int32, (DIFF_ROW_CHUNK, width), 1) + (j * tk + cs.start)
                        s = jnp.where(r >= c, s, NEG_BIG)
                    return s

                m_old = m_ref[mi, rs, :]
                m_new = jnp.maximum(m_old, jnp.max(load(slice(0, tk), tk), axis=-1, keepdims=True))
                alpha = jnp.exp2(m_old - m_new)
                psum = None
                for t in range(tk // LANES):
                    cs = slice(t * LANES, (t + 1) * LANES)
                    p = jnp.exp2(load(cs, LANES) - m_new)
                    psum = p if psum is None else psum + p
                    p_ref[par, mi, rs, cs] = p.astype(BF16)
                l_ref[mi, rs, :] = alpha * l_ref[mi, rs, :] + psum
                m_ref[mi, rs, :] = m_new
                alpha_ref[par, mi, rs, :] = alpha

    def accumulate(j, par):
        vs = v_ref[pl.ds(pl.multiple_of(j * tk, tk), tk), :]
        for mi in range(2):
            acc_ref[mi] = (jnp.tile(alpha_ref[par, mi], (1, DIFF_DV // LANES)) * acc_ref[mi]
                           + _dot(p_ref[par, mi], vs))

    scores(0, 0)

    @pl.when(n == 0)
    def _():
        softmax(0, 0, masked=True)
        accumulate(0, 0)

    @pl.when(n > 0)
    def _():
        scores(1, 1)
        softmax(0, 0, masked=False)

        def pair(t, carry):
            j = 2 * t + 1
            scores(j + 1, 0)
            softmax(j, 1, masked=False)
            accumulate(j - 1, 0)
            scores(j + 2, 1)
            softmax(j + 1, 0, masked=False)
            accumulate(j, 1)
            return carry

        lax.fori_loop(0, (n - 1) // 2, pair, 0)

        @pl.when(n % 2 == 0)
        def _():
            scores(n, 0)
            softmax(n - 1, 1, masked=False)
            accumulate(n - 2, 0)
            softmax(n, 0, masked=True)
            accumulate(n - 1, 1)
            accumulate(n, 0)

        @pl.when(n % 2 == 1)
        def _():
            softmax(n, 1, masked=True)
            accumulate(n - 1, 0)
            accumulate(n, 1)

    lf = lam_ref[...]
    lam = (jnp.exp(jnp.sum(lf[0:1] * lf[1:2], axis=-1, keepdims=True))
           - jnp.exp(jnp.sum(lf[2:3] * lf[3:4], axis=-1, keepdims=True)) + lambda_init)
    inv1 = 1.0 / jnp.sum(l_ref[0], axis=-1, keepdims=True)
    inv2 = 1.0 / jnp.sum(l_ref[1], axis=-1, keepdims=True)
    o = acc_ref[0] * inv1 - lam * (acc_ref[1] * inv2)
    y = o * lax.rsqrt(jnp.mean(o * o, axis=-1, keepdims=True) + LN_EPS)
    o_ref[...] = (y * g_ref[...] * (1.0 - lambda_init)).astype(BF16)


def _diff_attention(qkv, diff_lambda, subln_g, lambda_init, tq=512):
    S = qkv.shape[0]
    tq = min(tq, S // DIFF_KEY_TILES)
    tk = DIFF_KEY_TILES * tq
    qoff = 3 * RET_W // DIFF_DV
    koff = qoff + DIFF_W // DIFF_DV
    voff = koff + DIFF_W // DIFF_DV
    return pl.pallas_call(
        functools.partial(_diff_kernel, tq=tq, lambda_init=lambda_init),
        grid=(DIFF_HEADS, S // tq),
        in_specs=[pl.BlockSpec((4, DIFF_DK), lambda h, i: (0, 0)),
                  pl.BlockSpec((tq, DIFF_DV), lambda h, i: (i, qoff + h)),
                  pl.BlockSpec((S, DIFF_DV), lambda h, i: (0, koff + h), pipeline_mode=pl.Buffered(1)),
                  pl.BlockSpec((S, DIFF_DV), lambda h, i: (0, voff + h), pipeline_mode=pl.Buffered(1)),
                  pl.BlockSpec((1, DIFF_DV), lambda h, i: (0, h))],
        out_specs=pl.BlockSpec((tq, DIFF_DV), lambda h, i: (i, h)),
        out_shape=jax.ShapeDtypeStruct((S, DIFF_W), BF16),
        scratch_shapes=[pltpu.VMEM((tk, LANES), BF16),
                        pltpu.VMEM((2, 2, tq, tk), F32),
                        pltpu.VMEM((2, 2, tq, tk), BF16), pltpu.VMEM((2, 2, tq, LANES), F32),
                        pltpu.VMEM((2, tq, LANES), F32), pltpu.VMEM((2, tq, LANES), F32),
                        pltpu.VMEM((2, tq, DIFF_DV), F32)],
        compiler_params=_params("parallel", "arbitrary"),
    )(diff_lambda, qkv, qkv, qkv, subln_g.reshape(1, DIFF_W))


def _conv_kernel(cur_ref, halo_ref, w_ref, bdw_ref, g_ref, b_ref, o_ref, ext_ref, sh_ref, y_ref, *, tm):
    i = pl.program_id(0)
    ext_ref[0:CONV_HALO, :] = jnp.where(i > 0, halo_ref[...], 0.0)
    ext_ref[CONV_HALO:, :] = cur_ref[...]
    first = CONV_HALO - (CONV_WIDTH - 1)
    D = cur_ref.shape[1]
    nc = D // LANES
    for b in range(SUBLANES):
        n = tm + (CONV_WIDTH - 1 - b) // SUBLANES * SUBLANES
        for c in range(nc):
            sh_ref[b, c, 0:n, :] = ext_ref[first + b:first + b + n, c * LANES:(c + 1) * LANES]

    def channel_block(c, carry):
        acc = jnp.broadcast_to(bdw_ref[c], (tm, LANES))
        for j in range(CONV_WIDTH):
            b = j % SUBLANES
            acc = acc + w_ref[j, c] * sh_ref[b, c, j - b:j - b + tm, :]
        y_ref[c] = acc
        return carry

    lax.fori_loop(0, nc, channel_block, 0)

    total = y_ref[0]
    for c in range(1, nc):
        total = total + y_ref[c]
    mu = jnp.sum(total, axis=-1, keepdims=True) * (1.0 / D)
    sq = jnp.square(y_ref[0] - mu)
    for c in range(1, nc):
        sq = sq + jnp.square(y_ref[c] - mu)
    rstd = lax.rsqrt(jnp.sum(sq, axis=-1, keepdims=True) * (1.0 / D) + LN_EPS)
    for c in range(nc):
        cs = slice(c * LANES, (c + 1) * LANES)
        y = (y_ref[c] - mu) * rstd * g_ref[:, cs] + b_ref[:, cs]
        o_ref[:, cs] = _silu(y).astype(BF16)


def _conv_ln_silu(u, w_dw, b_dw, g, b, tm=256):
    S, D = u.shape
    per = tm // CONV_HALO
    nc = D // LANES
    fixed = lambda i: (0, 0)
    return pl.pallas_call(
        functools.partial(_conv_kernel, tm=tm),
        grid=(S // tm,),
        in_specs=[pl.BlockSpec((tm, D), lambda i: (i, 0)),
                  pl.BlockSpec((CONV_HALO, D), lambda i: (jnp.maximum(i * per - 1, 0), 0)),
                  pl.BlockSpec((CONV_WIDTH, nc, 1, LANES), lambda i: (0, 0, 0, 0)),
                  pl.BlockSpec((nc, 1, LANES), lambda i: (0, 0, 0)),
                  pl.BlockSpec((1, D), fixed), pl.BlockSpec((1, D), fixed)],
        out_specs=pl.BlockSpec((tm, D), lambda i: (i, 0)),
        out_shape=jax.ShapeDtypeStruct((S, D), BF16),
        scratch_shapes=[pltpu.VMEM((tm + CONV_HALO, D), F32),
                        pltpu.VMEM((SUBLANES, nc, tm + CONV_HALO - SUBLANES, LANES), F32),
                        pltpu.VMEM((nc, tm, LANES), F32)],
        compiler_params=_params("parallel"),
    )(u, u, w_dw.reshape(CONV_WIDTH, nc, 1, LANES), b_dw.reshape(nc, 1, LANES), g.reshape(1, D), b.reshape(1, D))


def _xattn_kernel(wq_ref, k_ref, v_ref, wo_ref, r_ref, g_ref, b_ref, of_ref, ob_ref):
    q = _dot(r_ref[...].astype(BF16), wq_ref[...]).astype(BF16)
    heads = []
    for h in range(XATTN_HEADS):
        cs = slice(h * XATTN_DH, (h + 1) * XATTN_DH)
        s = lax.dot_general(q[:, cs], k_ref[:, cs], _NT, preferred_element_type=F32)
        m = jnp.max(s, axis=-1, keepdims=True)
        p = jnp.exp2(s - m)
        p = p * (1.0 / jnp.sum(p, axis=-1, keepdims=True))
        heads.append(_dot(p.astype(BF16), v_ref[:, cs]).astype(BF16))
    o = jnp.concatenate(heads, axis=-1)
    z = ALPHA * r_ref[...] + _dot(o, wo_ref[...])
    y = _layer_norm(z, g_ref[...], b_ref[...])
    of_ref[...] = y
    ob_ref[...] = y.astype(BF16)


def _xattn(wq, k, v, wo, resid, g, b, tm=512):
    M, D = resid.shape
    row = lambda i: (i, 0)
    fixed = lambda i: (0, 0)
    resident = lambda shape: pl.BlockSpec(shape, fixed, pipeline_mode=pl.Buffered(1))
    return pl.pallas_call(
        _xattn_kernel,
        grid=(M // tm,),
        in_specs=[resident((D, D)), resident(k.shape), resident(v.shape), resident((D, D)),
                  pl.BlockSpec((tm, D), row),
                  pl.BlockSpec((1, D), fixed), pl.BlockSpec((1, D), fixed)],
        out_specs=[pl.BlockSpec((tm, D), row), pl.BlockSpec((tm, D), row)],
        out_shape=[jax.ShapeDtypeStruct((M, D), F32), jax.ShapeDtypeStruct((M, D), BF16)],
        compiler_params=_params("parallel"),
    )(wq, k, v, wo, resid, g.reshape(1, D), b.reshape(1, D))


def _lambda_init_for(layer):
    return 0.8 - 0.6 * math.exp(-0.3 * layer)


def kernel(x, mem, w_in, ret_gn_g, ret_gn_b, diff_lambda, diff_subln_g, w_mix_out, conv_w_pw1, conv_b_pw1, conv_w_dw, conv_b_dw, conv_ln_g, conv_ln_b, conv_w_pw2, conv_b_pw2, xattn_wq, xattn_wk, xattn_wv, xattn_wo, ffn_w_in, ffn_w_out, ln_g, ln_b):
    B, S, D = x.shape
    assert B == 1 and D == D_MODEL
    h = x[0]
    hb = h.astype(BF16)
    memb = mem[0].astype(BF16)
    zero_bias = jnp.zeros((D,), F32)
    for layer in range(DEPTH):
        last = layer == DEPTH - 1
        if layer % 2 == 0:
            e = layer // 2
            gate_blk, dq_blk = 3, 4
            col_scale = jnp.ones((1, w_in.shape[2]), F32).at[:, dq_blk * DIFF_W:(dq_blk + 1) * DIFF_W].set(
                DIFF_DK ** -0.5 * LOG2E)
            qkv = _matmul(hb, w_in, e, 6, BF16, src_block=lambda j: jnp.where(j >= gate_blk, j + 1, j),
                          col_scale=col_scale)
            gate = _matmul(hb, w_in, e, 1, F32, src_block=lambda j: j + gate_blk)
            ret = _retention(qkv, gate, ret_gn_g[e], ret_gn_b[e])
            dif = _diff_attention(qkv, diff_lambda[e], diff_subln_g[e], _lambda_init_for(layer + 1))
            wo = w_mix_out[e].astype(BF16)
            h, _ = _matmul_ln([ret, dif], [wo[:RET_W], wo[RET_W:]], zero_bias, h,
                              ln_g[layer, 0], ln_b[layer, 0], emit_bf16=False)
        else:
            o = layer // 2
            u = _matmul_glu(hb, conv_w_pw1[o].astype(BF16), conv_b_pw1[o])
            c = _conv_ln_silu(u, conv_w_dw[o], conv_b_dw[o], conv_ln_g[o], conv_ln_b[o])
            h, _ = _matmul_ln([c], [conv_w_pw2[o].astype(BF16)], conv_b_pw2[o], h,
                              ln_g[layer, 0], ln_b[layer, 0], emit_bf16=False)
        k = _matmul(memb, xattn_wk, layer, 2, BF16)
        v = _matmul(memb, xattn_wv, layer, 2, BF16)
        wq = (xattn_wq[layer] * (XATTN_DH ** -0.5 * LOG2E)).astype(BF16)
        h, hb = _xattn(wq, k, v, xattn_wo[layer].astype(BF16), h, ln_g[layer, 1], ln_b[layer, 1])
        h, hb = _ffn(hb, ffn_w_in[layer].astype(BF16), ffn_w_out[layer].astype(BF16), h,
                     ln_g[layer, 2], ln_b[layer, 2], emit_bf16=not last)
    return h[None]
```

```python
import functools
import math

import jax
import jax.numpy as jnp
from jax import lax
from jax.experimental import pallas as pl
from jax.experimental.pallas import tpu as pltpu

F32 = jnp.float32
BF16 = jnp.bfloat16

D_MODEL = 2048
DEPTH = 2
RET_HEADS = 8
RET_DK = 128
RET_CHUNK = 128
RET_W = 1024
DIFF_HEADS = 4
DIFF_DK = 128
DIFF_DV = 256
DIFF_W = 1024
DIFF_ROW_CHUNK = 32
DIFF_KEY_TILES = 2
CONV_WIDTH = 31
CONV_HALO = 32
XATTN_HEADS = 4
XATTN_DH = D_MODEL // XATTN_HEADS
D_FF = 5632
ALPHA = (2.0 * DEPTH) ** 0.25
LN_EPS = 1e-5
NEG_BIG = -1e30
LOG2E = math.log2(math.e)
LANES = 128
SUBLANES = 8

VMEM_LIMIT_BYTES = 56 * 1024 * 1024

_NT = (((1,), (1,)), ((), ()))
_TN = (((0,), (0,)), ((), ()))


def _params(*sem):
    return pltpu.CompilerParams(dimension_semantics=sem, vmem_limit_bytes=VMEM_LIMIT_BYTES)


def _dot(a, b):
    return jnp.dot(a, b, preferred_element_type=F32)


def _layer_norm(z, g, b):
    mu = jnp.mean(z, axis=-1, keepdims=True)
    d = z - mu
    var = jnp.mean(d * d, axis=-1, keepdims=True)
    return d * lax.rsqrt(var + LN_EPS) * g + b


def _silu(x):
    return x * jax.nn.sigmoid(x)


def _mm_kernel(x_ref, w_ref, *rest, scaled):
    o_ref, wb_ref = rest[-2:]

    @pl.when(pl.program_id(1) == 0)
    def _():
        w = w_ref[...]
        if scaled:
            w = w * rest[0][...]
        wb_ref[...] = w.astype(BF16)

    o_ref[...] = _dot(x_ref[...], wb_ref[...]).astype(o_ref.dtype)


def _matmul(x, w3, lead, n_blocks, out_dtype, src_block=lambda j: j, col_scale=None, tm=1024, tn=1024):
    M, K = x.shape
    tm = min(tm, M)
    scaled = col_scale is not None
    in_specs = [pl.BlockSpec((tm, K), lambda j, i: (i, 0)),
                pl.BlockSpec((None, K, tn), lambda j, i: (lead, 0, src_block(j)))]
    args = [x, w3]
    if scaled:
        in_specs.append(pl.BlockSpec((1, tn), lambda j, i: (0, src_block(j))))
        args.append(col_scale)
    return pl.pallas_call(
        functools.partial(_mm_kernel, scaled=scaled),
        grid=(n_blocks, M // tm),
        in_specs=in_specs,
        out_specs=pl.BlockSpec((tm, tn), lambda j, i: (i, j)),
        out_shape=jax.ShapeDtypeStruct((M, n_blocks * tn), out_dtype),
        scratch_shapes=[pltpu.VMEM((K, tn), BF16)],
        compiler_params=_params("parallel", "arbitrary"),
    )(*args)


def _mm_glu_kernel(x_ref, wa_ref, wb_ref, ba_ref, bb_ref, o_ref):
    x = x_ref[...]
    a = _dot(x, wa_ref[...]) + ba_ref[...]
    b = _dot(x, wb_ref[...]) + bb_ref[...]
    o_ref[...] = a * jax.nn.sigmoid(b)


def _matmul_glu(x, w, bias, tm=2048, tn=512):
    M, K = x.shape
    tm = min(tm, M)
    N = w.shape[1] // 2
    nb = N // tn
    bias2 = bias.reshape(1, 2 * N)
    return pl.pallas_call(
        _mm_glu_kernel,
        grid=(M // tm, nb),
        in_specs=[pl.BlockSpec((tm, K), lambda i, j: (i, 0)),
                  pl.BlockSpec((K, tn), lambda i, j: (0, j)),
                  pl.BlockSpec((K, tn), lambda i, j: (0, j + nb)),
                  pl.BlockSpec((1, tn), lambda i, j: (0, j)),
                  pl.BlockSpec((1, tn), lambda i, j: (0, j + nb))],
        out_specs=pl.BlockSpec((tm, tn), lambda i, j: (i, j)),
        out_shape=jax.ShapeDtypeStruct((M, N), F32),
        compiler_params=_params("parallel", "arbitrary"),
    )(x, w, w, bias2, bias2)


def _mm_ln_kernel(*refs, n_pairs, emit_bf16):
    xs = refs[:n_pairs]
    ws = refs[n_pairs:2 * n_pairs]
    bias_ref, r_ref, g_ref, b_ref = refs[2 * n_pairs:2 * n_pairs + 4]
    outs = refs[2 * n_pairs + 4:]
    acc = _dot(xs[0][...], ws[0][...])
    for x_ref, w_ref in zip(xs[1:], ws[1:]):
        acc = acc + _dot(x_ref[...], w_ref[...])
    z = ALPHA * r_ref[...] + acc + bias_ref[...]
    y = _layer_norm(z, g_ref[...], b_ref[...])
    outs[0][...] = y
    if emit_bf16:
        outs[1][...] = y.astype(BF16)


def _matmul_ln(xs, ws, bias, resid, g, b, emit_bf16=True, tm=512):
    M, N = resid.shape
    n = len(xs)
    row = lambda i: (i, 0)
    fixed = lambda i: (0, 0)
    in_specs = ([pl.BlockSpec((tm, x.shape[1]), row) for x in xs]
                + [pl.BlockSpec(w.shape, fixed) for w in ws]
                + [pl.BlockSpec((1, N), fixed), pl.BlockSpec((tm, N), row),
                   pl.BlockSpec((1, N), fixed), pl.BlockSpec((1, N), fixed)])
    out_specs = [pl.BlockSpec((tm, N), row)]
    out_shape = [jax.ShapeDtypeStruct((M, N), F32)]
    if emit_bf16:
        out_specs.append(pl.BlockSpec((tm, N), row))
        out_shape.append(jax.ShapeDtypeStruct((M, N), BF16))
    outs = pl.pallas_call(
        functools.partial(_mm_ln_kernel, n_pairs=n, emit_bf16=emit_bf16),
        grid=(M // tm,),
        in_specs=in_specs, out_specs=out_specs, out_shape=out_shape,
        compiler_params=_params("parallel"),
    )(*xs, *ws, bias.reshape(1, N), resid, g.reshape(1, N), b.reshape(1, N))
    return outs if emit_bf16 else (outs[0], None)


def _ffn_kernel(x_ref, wg_ref, wu_ref, wo_ref, r_ref, g_ref, b_ref, *rest, nrb, nf, ne, emit_bf16):
    outs, a_ref, acc_ref = rest[:-2], rest[-2], rest[-1]
    i = pl.program_id(0)
    f = pl.program_id(1)
    cur = i % 2
    te = acc_ref.shape[1] // ne
    active = i < nrb

    def up():
        x = x_ref[...]
        a_ref[...] = (_silu(_dot(x, wg_ref[...])) * _dot(x, wu_ref[...])).astype(BF16)

    def down():
        acc_ref[cur] += _dot(a_ref[...], wo_ref[...])

    def epilogue(e):
        z = ALPHA * r_ref[...] + acc_ref[1 - cur, e * te:(e + 1) * te, :]
        y = _layer_norm(z, g_ref[...], b_ref[...])
        outs[0][...] = y
        if emit_bf16:
            outs[1][...] = y.astype(BF16)

    for e in range(ne):
        pl.when(jnp.logical_and(i > 0, f == e + 1))(functools.partial(epilogue, e))

    @pl.when(jnp.logical_and(active, f == 0))
    def _():
        acc_ref[cur] = jnp.zeros(acc_ref.shape[1:], F32)
        up()

    @pl.when(jnp.logical_and(active, jnp.logical_and(f > 0, f < nf)))
    def _():
        down()
        up()

    @pl.when(jnp.logical_and(active, f == nf))
    def _():
        down()


def _ffn(xb, w_in, w_out, resid, g, b, emit_bf16=True, tm=1024, tf=512, ne=4):
    M, D = resid.shape
    tm = min(tm, M)
    nrb = M // tm
    nf = D_FF // tf
    te = tm // ne
    assert ne <= nf
    fixed = lambda i, f: (0, 0)
    chunk = lambda i, f: jnp.where(i < nrb, jnp.minimum(f, nf - 1), nf - 1)
    slab = lambda i, f: (jnp.where(i == 0, 0, (i - 1) * ne + jnp.clip(f - 1, 0, ne - 1)), 0)
    out_specs = [pl.BlockSpec((te, D), slab)]
    out_shape = [jax.ShapeDtypeStruct((M, D), F32)]
    if emit_bf16:
        out_specs.append(pl.BlockSpec((te, D), slab))
        out_shape.append(jax.ShapeDtypeStruct((M, D), BF16))
    outs = pl.pallas_call(
        functools.partial(_ffn_kernel, nrb=nrb, nf=nf, ne=ne, emit_bf16=emit_bf16),
        grid=(nrb + 1, nf + 1),
        in_specs=[pl.BlockSpec((tm, D), lambda i, f: (jnp.minimum(i, nrb - 1), 0)),
                  pl.BlockSpec((D, tf), lambda i, f: (0, chunk(i, f))),
                  pl.BlockSpec((D, tf), lambda i, f: (0, chunk(i, f) + nf)),
                  pl.BlockSpec((tf, D), lambda i, f: (jnp.where(i < nrb, jnp.clip(f - 1, 0, nf - 1), nf - 1), 0)),
                  pl.BlockSpec((te, D), slab),
                  pl.BlockSpec((1, D), fixed), pl.BlockSpec((1, D), fixed)],
        out_specs=out_specs, out_shape=out_shape,
        scratch_shapes=[pltpu.VMEM((tm, tf), BF16), pltpu.VMEM((2, tm, D), F32)],
        compiler_params=_params("arbitrary", "arbitrary"),
    )(xb, w_in, w_in, w_out, resid, g.reshape(1, D), b.reshape(1, D))
    return outs if emit_bf16 else (outs[0], None)


def _ret_kernel(x0_ref, xn_ref, wg_ref, q_ref, k_ref, v_ref, gg_ref, gb_ref, o_ref, state_ref, wgb_ref, gate0_ref,
                gate1_ref, *, nch):
    C = RET_CHUNK
    i = pl.program_id(0)

    @pl.when(i == 0)
    def _():
        state_ref[...] = jnp.zeros_like(state_ref)
        wgb_ref[...] = wg_ref[...].astype(BF16)
        gate0_ref[...] = _dot(x0_ref[...], wgb_ref[...])

    def run(gate_ref, next_gate_ref):
        next_gate_ref[...] = _dot(xn_ref[...], wgb_ref[...])
        scale = RET_DK ** -0.5
        row = lax.broadcasted_iota(jnp.int32, (C, C), 0).astype(F32)
        col = lax.broadcasted_iota(jnp.int32, (C, C), 1).astype(F32)
        dist = row - col
        for h in range(RET_HEADS):
            log_gamma = math.log1p(-(2.0 ** (-5.0 - h)))
            decay = jnp.where(dist >= 0, jnp.exp(log_gamma * jnp.maximum(dist, 0.0)), 0.0) * scale
            xi = jnp.exp(log_gamma * (row + 1.0))
            zeta = jnp.exp(log_gamma * (C - 1.0 - row)) * scale
            chunk_decay = math.exp(log_gamma * C)
            cs = slice(h * RET_DK, (h + 1) * RET_DK)
            gg = gg_ref[:, cs]
            gb = gb_ref[:, cs]
            for c in range(nch):
                rs = slice(c * C, (c + 1) * C)
                q = q_ref[rs, cs]
                k = k_ref[rs, cs]
                v = v_ref[rs, cs]
                state = state_ref[h]
                scores = lax.dot_general(q, k, _NT, preferred_element_type=F32) * decay
                y = _dot(scores.astype(BF16), v) + xi * _dot(q, state.astype(BF16))
                kz = (k.astype(F32) * zeta).astype(BF16)
                kv = lax.dot_general(kz, v, _TN, preferred_element_type=F32)
                state_ref[h] = state * chunk_decay + kv
                mu = jnp.mean(y, axis=-1, keepdims=True)
                d = y - mu
                var = jnp.mean(d * d, axis=-1, keepdims=True)
                yn = d * lax.rsqrt(var + LN_EPS) * gg + gb
                o_ref[rs, cs] = (_silu(gate_ref[rs, cs]) * yn).astype(BF16)

    pl.when(i % 2 == 0)(functools.partial(run, gate0_ref, gate1_ref))
    pl.when(i % 2 == 1)(functools.partial(run, gate1_ref, gate0_ref))


def _retention(xb, w3, lead, gate_blk, qkv, gn_g, gn_b, tr=512):
    S, D = xb.shape
    tr = min(tr, S)
    fixed = lambda i: (0, 0)
    return pl.pallas_call(
        functools.partial(_ret_kernel, nch=tr // RET_CHUNK),
        grid=(S // tr,),
        in_specs=[pl.BlockSpec((tr, D), lambda i: (0, 0)),
                  pl.BlockSpec((tr, D), lambda i: (jnp.minimum(i + 1, S // tr - 1), 0)),
                  pl.BlockSpec((None, D, RET_W), lambda i: (lead, 0, gate_blk), pipeline_mode=pl.Buffered(1)),
                  pl.BlockSpec((tr, RET_W), lambda i: (i, 0)),
                  pl.BlockSpec((tr, RET_W), lambda i: (i, 1)),
                  pl.BlockSpec((tr, RET_W), lambda i: (i, 2)),
                  pl.BlockSpec((1, RET_W), fixed), pl.BlockSpec((1, RET_W), fixed)],
        out_specs=pl.BlockSpec((tr, RET_W), lambda i: (i, 0)),
        out_shape=jax.ShapeDtypeStruct((S, RET_W), BF16),
        scratch_shapes=[pltpu.VMEM((RET_HEADS, RET_DK, RET_DK), F32),
                        pltpu.VMEM((D, RET_W), BF16), pltpu.VMEM((tr, RET_W), F32), pltpu.VMEM((tr, RET_W), F32)],
        compiler_params=_params("arbitrary"),
    )(xb, xb, w3, qkv, qkv, qkv, gn_g.reshape(1, RET_W), gn_b.reshape(1, RET_W))


def _diff_kernel(lam_ref, q_ref, k_ref, v_ref, g_ref, o_ref, kb_ref, s_ref, p_ref, alpha_ref,
                 m_ref, l_ref, acc_ref, *, tq, lambda_init):
    h = pl.program_id(0)
    i = pl.program_id(1)
    tk = DIFF_KEY_TILES * tq
    n = i // DIFF_KEY_TILES
    m_ref[...] = jnp.full(m_ref.shape, NEG_BIG, F32)
    l_ref[...] = jnp.zeros_like(l_ref)
    acc_ref[...] = jnp.zeros_like(acc_ref)

    head = (h + 1).astype(F32)
    slope = jnp.exp2(jnp.full((tk, LANES), -8.0 / DIFF_HEADS, F32) * head) * LOG2E
    lane = lax.broadcasted_iota(jnp.int32, (tk, LANES), 1)

    def split3(x, first_lane, lanes):
        hi = x.astype(BF16).astype(F32)
        mid = (x - hi).astype(BF16).astype(F32)
        lo = x - hi - mid
        pieces = jnp.where(lanes == first_lane, hi,
                           jnp.where(lanes == first_lane + 1, mid,
                                     jnp.where(lanes == first_lane + 2, lo, 0.0)))
        return pieces.astype(BF16)

    @pl.when(i == 0)
    def _():
        within = slope * lax.broadcasted_iota(jnp.int32, (tk, LANES), 0).astype(F32)
        kb_ref[...] = jnp.where(jnp.logical_and(lane >= 3, lane < 6), 1.0,
                                split3(within, 0, lane).astype(F32)).astype(BF16)

    q_ones = jnp.where(lane[:SUBLANES] < 3, 1.0, 0.0)

    def scores(j, par):
        ks = k_ref[pl.ds(pl.multiple_of(j * tk, tk), tk), :]
        tile_bias = split3(slope[:SUBLANES] * (j * tk - i * tq).astype(F32), 3, lane[:SUBLANES]).astype(F32)
        qb = jnp.tile(q_ones + tile_bias, (tq // SUBLANES, 1)).astype(BF16)
        kb = kb_ref[...]
        for mi in range(2):
            ms = slice(mi * DIFF_DK, (mi + 1) * DIFF_DK)
            qa = jnp.concatenate([q_ref[:, ms], qb], axis=1)
            ka = jnp.concatenate([ks[:, ms], kb], axis=1)
            s_ref[par, mi] = lax.dot_general(qa, ka, _NT, preferred_element_type=F32)

    def softmax(j, par, masked):
        for mi in range(2):
            for r0 in range(0, tq, DIFF_ROW_CHUNK):
                rs = slice(r0, r0 + DIFF_ROW_CHUNK)
                def load(cs, width):
                    s = s_ref[par, mi, rs, cs]
                    if masked:
                        r = lax.broadcasted_iota(jnp.int32, (DIFF_ROW_CHUNK, width), 0) + (r0 + i * tq)
                        c = lax.broadcasted_iota(jnp.int32, (DIFF_ROW_CHUNK, width), 1) + (j * tk + cs.start)
                        s = jnp.where(r >= c, s, NEG_BIG)
                    return s

                m_old = m_ref[mi, rs, :]
                m_new = jnp.maximum(m_old, jnp.max(load(slice(0, tk), tk), axis=-1, keepdims=True))
                alpha = jnp.exp2(m_old - m_new)
                psum = None
                for t in range(tk // LANES):
                    cs = slice(t * LANES, (t + 1) * LANES)
                    p = jnp.exp2(load(cs, LANES) - m_new)
                    psum = p if psum is None else psum + p
                    p_ref[par, mi, rs, cs] = p.astype(BF16)
                l_ref[mi, rs, :] = alpha * l_ref[mi, rs, :] + psum
                m_ref[mi, rs, :] = m_new
                alpha_ref[par, mi, rs, :] = alpha

    def accumulate(j, par):
        vs = v_ref[pl.ds(pl.multiple_of(j * tk, tk), tk), :]
        for mi in range(2):
            acc_ref[mi] = (jnp.tile(alpha_ref[par, mi], (1, DIFF_DV // LANES)) * acc_ref[mi]
                           + _dot(p_ref[par, mi], vs))

    scores(0, 0)

    @pl.when(n == 0)
    def _():
        softmax(0, 0, masked=True)
        accumulate(0, 0)

    @pl.when(n > 0)
    def _():
        scores(1, 1)
        softmax(0, 0, masked=False)

        def pair(t, carry):
            j = 2 * t + 1
            scores(j + 1, 0)
            softmax(j, 1, masked=False)
            accumulate(j - 1, 0)
            scores(j + 2, 1)
            softmax(j + 1, 0, masked=False)
            accumulate(j, 1)
            return carry

        lax.fori_loop(0, (n - 1) // 2, pair, 0)

        @pl.when(n % 2 == 0)
        def _():
            scores(n, 0)
            softmax(n - 1, 1, masked=False)
            accumulate(n - 2, 0)
            softmax(n, 0, masked=True)
            accumulate(n - 1, 1)
            accumulate(n, 0)

        @pl.when(n % 2 == 1)
        def _():
            softmax(n, 1, masked=True)
            accumulate(n - 1, 0)
            accumulate(n, 1)

    lf = lam_ref[...]
    lam = (jnp.exp(jnp.sum(lf[0:1] * lf[1:2], axis=-1, keepdims=True))
           - jnp.exp(jnp.sum(lf[2:3] * lf[3:4], axis=-1, keepdims=True)) + lambda_init)
    inv1 = 1.0 / jnp.sum(l_ref[0], axis=-1, keepdims=True)
    inv2 = 1.0 / jnp.sum(l_ref[1], axis=-1, keepdims=True)
    o = acc_ref[0] * inv1 - lam * (acc_ref[1] * inv2)
    y = o * lax.rsqrt(jnp.mean(o * o, axis=-1, keepdims=True) + LN_EPS)
    o_ref[...] = (y * g_ref[...] * (1.0 - lambda_init)).astype(BF16)


def _diff_attention(qkv, diff_lambda, subln_g, lambda_init, tq=512):
    S = qkv.shape[0]
    tq = min(tq, S // DIFF_KEY_TILES)
    tk = DIFF_KEY_TILES * tq
    qoff = 3 * RET_W // DIFF_DV
    koff = qoff + DIFF_W // DIFF_DV
    voff = koff + DIFF_W // DIFF_DV
    return pl.pallas_call(
        functools.partial(_diff_kernel, tq=tq, lambda_init=lambda_init),
        grid=(DIFF_HEADS, S // tq),
        in_specs=[pl.BlockSpec((4, DIFF_DK), lambda h, i: (0, 0)),
                  pl.BlockSpec((tq, DIFF_DV), lambda h, i: (i, qoff + h)),
                  pl.BlockSpec((S, DIFF_DV), lambda h, i: (0, koff + h), pipeline_mode=pl.Buffered(1)),
                  pl.BlockSpec((S, DIFF_DV), lambda h, i: (0, voff + h), pipeline_mode=pl.Buffered(1)),
                  pl.BlockSpec((1, DIFF_DV), lambda h, i: (0, h))],
        out_specs=pl.BlockSpec((tq, DIFF_DV), lambda h, i: (i, h)),
        out_shape=jax.ShapeDtypeStruct((S, DIFF_W), BF16),
        scratch_shapes=[pltpu.VMEM((tk, LANES), BF16),
                        pltpu.VMEM((2, 2, tq, tk), F32),
                        pltpu.VMEM((2, 2, tq, tk), BF16), pltpu.VMEM((2, 2, tq, LANES), F32),
                        pltpu.VMEM((2, tq, LANES), F32), pltpu.VMEM((2, tq, LANES), F32),
                        pltpu.VMEM((2, tq, DIFF_DV), F32)],
        compiler_params=_params("parallel", "arbitrary"),
    )(diff_lambda, qkv, qkv, qkv, subln_g.reshape(1, DIFF_W))


def _conv_kernel(cur_ref, halo_ref, w_ref, bdw_ref, g_ref, b_ref, o_ref, ext_ref, sh_ref, y_ref, *, tm):
    i = pl.program_id(0)
    ext_ref[0:CONV_HALO, :] = jnp.where(i > 0, halo_ref[...], 0.0)
    ext_ref[CONV_HALO:, :] = cur_ref[...]
    first = CONV_HALO - (CONV_WIDTH - 1)
    D = cur_ref.shape[1]
    nc = D // LANES
    for b in range(SUBLANES):
        n = tm + (CONV_WIDTH - 1 - b) // SUBLANES * SUBLANES
        for c in range(nc):
            sh_ref[b, c, 0:n, :] = ext_ref[first + b:first + b + n, c * LANES:(c + 1) * LANES]

    def channel_block(c, carry):
        acc = jnp.broadcast_to(bdw_ref[c], (tm, LANES))
        for j in range(CONV_WIDTH):
            b = j % SUBLANES
            acc = acc + w_ref[j, c] * sh_ref[b, c, j - b:j - b + tm, :]
        y_ref[c] = acc
        return carry

    lax.fori_loop(0, nc, channel_block, 0)

    total = y_ref[0]
    for c in range(1, nc):
        total = total + y_ref[c]
    mu = jnp.sum(total, axis=-1, keepdims=True) * (1.0 / D)
    sq = jnp.square(y_ref[0] - mu)
    for c in range(1, nc):
        sq = sq + jnp.square(y_ref[c] - mu)
    rstd = lax.rsqrt(jnp.sum(sq, axis=-1, keepdims=True) * (1.0 / D) + LN_EPS)
    for c in range(nc):
        cs = slice(c * LANES, (c + 1) * LANES)
        y = (y_ref[c] - mu) * rstd * g_ref[:, cs] + b_ref[:, cs]
        o_ref[:, cs] = _silu(y).astype(BF16)


def _conv_ln_silu(u, w_dw, b_dw, g, b, tm=256):
    S, D = u.shape
    per = tm // CONV_HALO
    nc = D // LANES
    fixed = lambda i: (0, 0)
    return pl.pallas_call(
        functools.partial(_conv_kernel, tm=tm),
        grid=(S // tm,),
        in_specs=[pl.BlockSpec((tm, D), lambda i: (i, 0)),
                  pl.BlockSpec((CONV_HALO, D), lambda i: (jnp.maximum(i * per - 1, 0), 0)),
                  pl.BlockSpec((CONV_WIDTH, nc, 1, LANES), lambda i: (0, 0, 0, 0)),
                  pl.BlockSpec((nc, 1, LANES), lambda i: (0, 0, 0)),
                  pl.BlockSpec((1, D), fixed), pl.BlockSpec((1, D), fixed)],
        out_specs=pl.BlockSpec((tm, D), lambda i: (i, 0)),
        out_shape=jax.ShapeDtypeStruct((S, D), BF16),
        scratch_shapes=[pltpu.VMEM((tm + CONV_HALO, D), F32),
                        pltpu.VMEM((SUBLANES, nc, tm + CONV_HALO - SUBLANES, LANES), F32),
                        pltpu.VMEM((nc, tm, LANES), F32)],
        compiler_params=_params("parallel"),
    )(u, u, w_dw.reshape(CONV_WIDTH, nc, 1, LANES), b_dw.reshape(nc, 1, LANES), g.reshape(1, D), b.reshape(1, D))


def _xattn_kernel(wq_ref, k_ref, v_ref, wo_ref, r_ref, g_ref, b_ref, of_ref, ob_ref):
    q = _dot(r_ref[...].astype(BF16), wq_ref[...]).astype(BF16)
    heads = []
    for h in range(XATTN_HEADS):
        cs = slice(h * XATTN_DH, (h + 1) * XATTN_DH)
        s = lax.dot_general(q[:, cs], k_ref[:, cs], _NT, preferred_element_type=F32)
        m = jnp.max(s, axis=-1, keepdims=True)
        p = jnp.exp2(s - m)
        p = p * (1.0 / jnp.sum(p, axis=-1, keepdims=True))
        heads.append(_dot(p.astype(BF16), v_ref[:, cs]).astype(BF16))
    o = jnp.concatenate(heads, axis=-1)
    z = ALPHA * r_ref[...] + _dot(o, wo_ref[...])
    y = _layer_norm(z, g_ref[...], b_ref[...])
    of_ref[...] = y
    ob_ref[...] = y.astype(BF16)


def _xattn(wq, k, v, wo, resid, g, b, tm=512):
    M, D = resid.shape
    row = lambda i: (i, 0)
    fixed = lambda i: (0, 0)
    resident = lambda shape: pl.BlockSpec(shape, fixed, pipeline_mode=pl.Buffered(1))
    return pl.pallas_call(
        _xattn_kernel,
        grid=(M // tm,),
        in_specs=[resident((D, D)), resident(k.shape), resident(v.shape), resident((D, D)),
                  pl.BlockSpec((tm, D), row),
                  pl.BlockSpec((1, D), fixed), pl.BlockSpec((1, D), fixed)],
        out_specs=[pl.BlockSpec((tm, D), row), pl.BlockSpec((tm, D), row)],
        out_shape=[jax.ShapeDtypeStruct((M, D), F32), jax.ShapeDtypeStruct((M, D), BF16)],
        compiler_params=_params("parallel"),
    )(wq, k, v, wo, resid, g.reshape(1, D), b.reshape(1, D))


def _lambda_init_for(layer):
    return 0.8 - 0.6 * math.exp(-0.3 * layer)


def kernel(x, mem, w_in, ret_gn_g, ret_gn_b, diff_lambda, diff_subln_g, w_mix_out, conv_w_pw1, conv_b_pw1, conv_w_dw, conv_b_dw, conv_ln_g, conv_ln_b, conv_w_pw2, conv_b_pw2, xattn_wq, xattn_wk, xattn_wv, xattn_wo, ffn_w_in, ffn_w_out, ln_g, ln_b):
    B, S, D = x.shape
    assert B == 1 and D == D_MODEL
    h = x[0]
    hb = h.astype(BF16)
    memb = mem[0].astype(BF16)
    zero_bias = jnp.zeros((D,), F32)
    for layer in range(DEPTH):
        last = layer == DEPTH - 1
        if layer % 2 == 0:
            e = layer // 2
            gate_blk, dq_blk = 3, 4
            col_scale = jnp.ones((1, w_in.shape[2]), F32).at[:, dq_blk * DIFF_W:(dq_blk + 1) * DIFF_W].set(
                DIFF_DK ** -0.5 * LOG2E)
            qkv = _matmul(hb, w_in, e, 6, BF16, src_block=lambda j: jnp.where(j >= gate_blk, j + 1, j),
                          col_scale=col_scale)
            ret = _retention(hb, w_in, e, gate_blk, qkv, ret_gn_g[e], ret_gn_b[e])
            dif = _diff_attention(qkv, diff_lambda[e], diff_subln_g[e], _lambda_init_for(layer + 1))
            wo = w_mix_out[e].astype(BF16)
            h, _ = _matmul_ln([ret, dif], [wo[:RET_W], wo[RET_W:]], zero_bias, h,
                              ln_g[layer, 0], ln_b[layer, 0], emit_bf16=False)
        else:
            o = layer // 2
            u = _matmul_glu(hb, conv_w_pw1[o].astype(BF16), conv_b_pw1[o])
            c = _conv_ln_silu(u, conv_w_dw[o], conv_b_dw[o], conv_ln_g[o], conv_ln_b[o])
            h, _ = _matmul_ln([c], [conv_w_pw2[o].astype(BF16)], conv_b_pw2[o], h,
                              ln_g[layer, 0], ln_b[layer, 0], emit_bf16=False)
        k = _matmul(memb, xattn_wk, layer, 2, BF16)
        v = _matmul(memb, xattn_wv, layer, 2, BF16)
        wq = (xattn_wq[layer] * (XATTN_DH ** -0.5 * LOG2E)).astype(BF16)
        h, hb = _xattn(wq, k, v, xattn_wo[layer].astype(BF16), h, ln_g[layer, 1], ln_b[layer, 1])
        h, hb = _ffn(hb, ffn_w_in[layer].astype(BF16), ffn_w_out[layer].astype(BF16), h,
                     ln_g[layer, 2], ln_b[layer, 2], emit_bf16=not last)
    return h[None]
```
